```python
import math
import jax
import jax.numpy as jnp
from jax import lax
import numpy as np

D_MODEL = 1024
BATCH = 16
SEQ = 256
DEPTH = 4
DEC_BATCH = 8
DEC_SEQ = 2048
PAST_LEN = 512

GRID_W = 64
N_MIXERS = 3
N_LAYERS_A = (DEPTH + 2) // N_MIXERS
N_LAYERS_B = (DEPTH + 1) // N_MIXERS
N_LAYERS_C = DEPTH // N_MIXERS
D_FF = 4 * D_MODEL
N_MOD = 6
EPS = 1e-6
ROPE_THETA = 10000.0
Q_BLOCK = 128
H_A = 8
DK_A = 128
DV_A = 128
WK_A = H_A * DK_A
WV_A = H_A * DV_A
CONV_K = 3
CHUNK = 64
H_B = 8
Q_LORA = 384
KV_LORA = 256
NOPE_B = 128
ROPE_B = 64
V_B = 128
H_C = 8
KVH_C = 2
HD_C = 128

kernel_name = 'hybrid_gdn_mla_gqa_diffusion_step'


def rms_norm(x, g):
    xf = x.astype(jnp.float32)
    y = xf * lax.rsqrt(jnp.mean(xf * xf, axis=-1, keepdims=True) + EPS)
    return (y * g.astype(jnp.float32)).astype(x.dtype)


def l2_norm(x):
    xf = x.astype(jnp.float32)
    return (xf * lax.rsqrt(jnp.sum(xf * xf, axis=-1, keepdims=True) + EPS)).astype(x.dtype)


def adaln(cond, w_mod, b_mod):
    m = jax.nn.silu(cond) @ w_mod + b_mod
    return jnp.split(m[:, None, :], N_MOD, axis=-1)


def modulated_norm(x, g, shift, scale):
    return rms_norm(x, g) * (1 + scale) + shift


def sq_relu_mlp(h, w_in, w_out):
    return jnp.square(jax.nn.relu(h @ w_in)) @ w_out


def grid_positions(rows):
    row = jnp.repeat(jnp.arange(rows, dtype=jnp.float32), GRID_W)
    col = jnp.tile(jnp.arange(GRID_W, dtype=jnp.float32), rows)
    return row, col


def rope_1d(x, pos):
    half = x.shape[-1] // 2
    freqs = ROPE_THETA ** (-jnp.arange(half, dtype=jnp.float32) / half)
    ang = pos[:, None] * freqs[None, :]
    cos = jnp.cos(ang)[None, :, None, :]
    sin = jnp.sin(ang)[None, :, None, :]
    x1 = x[..., :half].astype(jnp.float32)
    x2 = x[..., half:].astype(jnp.float32)
    return jnp.concatenate([x1 * cos - x2 * sin, x1 * sin + x2 * cos], axis=-1).astype(x.dtype)


def axial_rope(x, row_pos, col_pos):
    half = x.shape[-1] // 2
    return jnp.concatenate([rope_1d(x[..., :half], row_pos), rope_1d(x[..., half:], col_pos)], axis=-1)


def block_attention(q, k, v):
    b, sq, h, dk = q.shape
    hk, dv = k.shape[2], v.shape[-1]
    grp = h // hk
    scale = dk ** -0.5
    qb = jnp.moveaxis(q.reshape(b, sq // Q_BLOCK, Q_BLOCK, hk, grp, dk), 1, 0)

    def one_block(q_blk):
        s = jnp.einsum('bqkgd,bskd->bkgqs', q_blk, k, preferred_element_type=jnp.float32) * scale
        p = jax.nn.softmax(s, axis=-1).astype(v.dtype)
        return jnp.einsum('bkgqs,bskd->bqkgd', p, v)

    o = lax.map(one_block, qb)
    return jnp.moveaxis(o, 0, 1).reshape(b, sq, h, dv)


def centred_depthwise_conv(x, w):
    pad = (CONV_K - 1) // 2
    return lax.conv_general_dilated(x, w[:, None, :].astype(x.dtype), window_strides=(1,),
                                    padding=[(pad, pad)], dimension_numbers=('NWC', 'WIO', 'NWC'),
                                    feature_group_count=x.shape[-1])


def gated_delta_chunked(q, k, v, g, beta, s0):
    f32 = jnp.float32
    b, n_tok, h, _ = q.shape
    dv = v.shape[-1]
    n = n_tok // CHUNK

    def chunks(t):
        t = t.astype(f32).reshape((b, n, CHUNK, h) + t.shape[3:])
        return jnp.moveaxis(t, 3, 1)

    qc, kc, vc, gc, bc = map(chunks, (q, k, v, g, beta))
    gcum = jnp.cumsum(gc, axis=-1)
    idx = jnp.arange(CHUNK)
    incl = idx[:, None] >= idx[None, :]
    strict = idx[:, None] > idx[None, :]
    decay = jnp.exp(jnp.where(incl, gcum[..., :, None] - gcum[..., None, :], -jnp.inf))
    kb = kc * bc[..., None]
    a_mat = jnp.where(strict, jnp.einsum('bhncd,bhnmd->bhncm', kb, kc) * decay, 0.0)
    lower = a_mat + jnp.eye(CHUNK, dtype=f32)
    rhs = jnp.concatenate([vc * bc[..., None], kb * jnp.exp(gcum)[..., None]], axis=-1)
    sol = lax.linalg.triangular_solve(lower, rhs, left_side=True, lower=True, unit_diagonal=True)
    u, w = sol[..., :dv], sol[..., dv:]
    qk = jnp.einsum('bhncd,bhnmd->bhncm', qc, kc) * decay
    q_dec = qc * jnp.exp(gcum)[..., None]
    k_dec = kc * jnp.exp(gcum[..., -1:] - gcum)[..., None]
    g_tot = jnp.exp(gcum[..., -1])
    xs = tuple(jnp.moveaxis(t, 2, 0) for t in (u, w, qk, q_dec, k_dec, g_tot))

    def step(state, xs_i):
        u_i, w_i, qk_i, qd_i, kd_i, gt_i = xs_i
        v_new = u_i - jnp.einsum('bhcd,bhde->bhce', w_i, state)
        o_i = jnp.einsum('bhcd,bhde->bhce', qd_i, state) + jnp.einsum('bhcm,bhme->bhce', qk_i, v_new)
        state = state * gt_i[..., None, None] + jnp.einsum('bhcd,bhce->bhde', kd_i, v_new)
        return state, o_i

    s_fin, o = lax.scan(step, s0.astype(f32), xs)
    o = jnp.transpose(o, (1, 0, 3, 2, 4)).reshape(b, n_tok, h, dv)
    return o.astype(v.dtype), s_fin


def gdn_mixer(h, p, s0_fwd, s0_bwd):
    w_in, conv_w, a_log, dt_bias, out_norm, w_out = p
    b, s, _ = h.shape
    proj = h @ w_in
    n_qkv = 2 * WK_A + WV_A
    qkv = jax.nn.silu(centred_depthwise_conv(proj[..., :n_qkv], conv_w))
    z = proj[..., n_qkv:n_qkv + WV_A]
    gb = proj[..., n_qkv + WV_A:].astype(jnp.float32).reshape(b, s, 2, 2, H_A)
    q = l2_norm(qkv[..., :WK_A].reshape(b, s, H_A, DK_A)) * (DK_A ** -0.5)
    k = l2_norm(qkv[..., WK_A:2 * WK_A].reshape(b, s, H_A, DK_A))
    v = qkv[..., 2 * WK_A:].reshape(b, s, H_A, DV_A)
    g = -jnp.exp(a_log.astype(jnp.float32)) * jax.nn.softplus(gb[:, :, 0] + dt_bias.astype(jnp.float32))
    beta = jax.nn.sigmoid(gb[:, :, 1])
    o_f, s_f = gated_delta_chunked(q, k, v, g[:, :, 0], beta[:, :, 0], s0_fwd)
    rev = lambda t: jnp.flip(t, axis=1)
    o_b, s_b = gated_delta_chunked(rev(q), rev(k), rev(v), rev(g[:, :, 1]), rev(beta[:, :, 1]), s0_bwd)
    o = rms_norm(o_f + rev(o_b), out_norm) * jax.nn.silu(z).reshape(b, s, H_A, DV_A)
    return o.reshape(b, s, WV_A) @ w_out, s_f, s_b


def mla_project(h, p):
    w_down, q_lat_g, kv_lat_g, w_uq, w_ukv, qn_nope, qn_rope, kn_nope, kn_rope, w_out = p
    b, s, _ = h.shape
    proj = h @ w_down
    cq = rms_norm(proj[..., :Q_LORA], q_lat_g)
    ckv = rms_norm(proj[..., Q_LORA:Q_LORA + KV_LORA], kv_lat_g)
    k_rope = rms_norm(proj[..., Q_LORA + KV_LORA:], kn_rope)
    q = (cq @ w_uq).reshape(b, s, H_B, NOPE_B + ROPE_B)
    return rms_norm(q[..., :NOPE_B], qn_nope), rms_norm(q[..., NOPE_B:], qn_rope), ckv, k_rope


def mla_keys_values(ckv, k_rope, p):
    w_ukv, kn_nope = p[4], p[7]
    b, s, _ = ckv.shape
    kv = (ckv @ w_ukv).reshape(b, s, H_B, NOPE_B + V_B)
    k_nope = rms_norm(kv[..., :NOPE_B], kn_nope)
    k_rope_h = jnp.broadcast_to(k_rope[:, :, None, :], (b, s, H_B, ROPE_B))
    return jnp.concatenate([k_nope, k_rope_h], axis=-1), kv[..., NOPE_B:]


def mla_context(h, p):
    b, s, _ = h.shape
    q_nope, q_rope, ckv, k_rope = mla_project(h, p)
    k, v = mla_keys_values(ckv, k_rope, p)
    o = block_attention(jnp.concatenate([q_nope, q_rope], axis=-1), k, v)
    return o.reshape(b, s, H_B * V_B) @ p[9], ckv, k_rope


def mla_latent(h, ckv_ctx, krope_ctx, row_pos, col_pos, p):
    b, s, _ = h.shape
    q_nope, q_rope, ckv, k_rope = mla_project(h, p)
    q_rope = axial_rope(q_rope, row_pos, col_pos)
    k_rope = axial_rope(k_rope[:, :, None, :], row_pos, col_pos)[:, :, 0]
    k_lat, v_lat = mla_keys_values(ckv, k_rope, p)
    k_ctx, v_ctx = mla_keys_values(ckv_ctx, krope_ctx, p)
    o = block_attention(jnp.concatenate([q_nope, q_rope], axis=-1),
                        jnp.concatenate([k_ctx, k_lat], axis=1), jnp.concatenate([v_ctx, v_lat], axis=1))
    return o.reshape(b, s, H_B * V_B) @ p[9]


def gqa_project(h, p):
    w_in, q_g, k_g, _ = p
    b, s, _ = h.shape
    proj = h @ w_in
    q = rms_norm(proj[..., :H_C * HD_C].reshape(b, s, H_C, HD_C), q_g)
    k = rms_norm(proj[..., H_C * HD_C:(H_C + KVH_C) * HD_C].reshape(b, s, KVH_C, HD_C), k_g)
    v = proj[..., (H_C + KVH_C) * HD_C:].reshape(b, s, KVH_C, HD_C)
    return q, k, v


def gqa_context(h, p):
    b, s, _ = h.shape
    q, k, v = gqa_project(h, p)
    o = block_attention(q, k, v)
    return o.reshape(b, s, H_C * HD_C) @ p[3], k, v


def gqa_latent(h, k_ctx, v_ctx, row_pos, col_pos, p):
    b, s, _ = h.shape
    q, k, v = gqa_project(h, p)
    q = axial_rope(q, row_pos, col_pos)
    k = axial_rope(k, row_pos, col_pos)
    o = block_attention(q, jnp.concatenate([k_ctx, k], axis=1), jnp.concatenate([v_ctx, v], axis=1))
    return o.reshape(b, s, H_C * HD_C) @ p[3]


def setup_inputs(seed: int = 0) -> dict:
    key = jax.random.key(seed)
    keys = iter(jax.random.split(key, 64))
    f32 = jnp.float32
    d = D_MODEL

    def normal(shape, scale):
        return scale * jax.random.normal(next(keys), shape, f32)

    def gain(shape):
        return 1.0 + 0.05 * jax.random.normal(next(keys), shape, f32)

    a_log = jnp.log(jax.random.uniform(next(keys), (N_LAYERS_A, 2, H_A), f32, 1.0, 16.0))
    dt = jnp.exp(jax.random.uniform(next(keys), (N_LAYERS_A, 2, H_A), f32, math.log(1e-3), math.log(1e-1)))
    dt_bias = dt + jnp.log(-jnp.expm1(-dt))
    return {
        'x_prompt': normal((BATCH, SEQ, d), 1.0),
        'x_sample': normal((DEC_BATCH, DEC_SEQ, d), 1.0),
        'state_gdn_fwd': normal((DEC_BATCH, N_LAYERS_A, H_A, DK_A, DV_A), 0.2),
        'state_gdn_bwd': normal((DEC_BATCH, N_LAYERS_A, H_A, DK_A, DV_A), 0.2),
        'cache_mla_ckv': normal((DEC_BATCH, N_LAYERS_B, PAST_LEN, KV_LORA), 1.0),
        'cache_mla_krope': normal((DEC_BATCH, N_LAYERS_B, PAST_LEN, ROPE_B), 1.0),
        'cache_gqa_k': normal((DEC_BATCH, N_LAYERS_C, PAST_LEN, KVH_C, HD_C), 1.0),
        'cache_gqa_v': normal((DEC_BATCH, N_LAYERS_C, PAST_LEN, KVH_C, HD_C), 1.0),
        'c': normal((DEC_BATCH, d), 1.0),
        'c_ctx': normal((d,), 1.0),
        'norm_mix': gain((DEPTH, d)),
        'norm_mlp': gain((DEPTH, d)),
        'w_mod': normal((DEPTH, d, N_MOD * d), 0.5 * d ** -0.5),
        'b_mod': normal((DEPTH, N_MOD * d), 0.02),
        'w_mlp_in': normal((DEPTH, d, D_FF), d ** -0.5),
        'w_mlp_out': normal((DEPTH, D_FF, d), D_FF ** -0.5),
        'gdn_w_in': normal((N_LAYERS_A, d, 2 * WK_A + 2 * WV_A + 4 * H_A), d ** -0.5),
        'gdn_conv': normal((N_LAYERS_A, CONV_K, 2 * WK_A + WV_A), CONV_K ** -0.5),
        'gdn_a_log': a_log,
        'gdn_dt_bias': dt_bias,
        'gdn_out_norm': gain((N_LAYERS_A, DV_A)),
        'gdn_w_out': normal((N_LAYERS_A, WV_A, d), WV_A ** -0.5),
        'mla_w_down': normal((N_LAYERS_B, d, Q_LORA + KV_LORA + ROPE_B), d ** -0.5),
        'mla_q_lat_norm': gain((N_LAYERS_B, Q_LORA)),
        'mla_kv_lat_norm': gain((N_LAYERS_B, KV_LORA)),
        'mla_w_uq': normal((N_LAYERS_B, Q_LORA, H_B * (NOPE_B + ROPE_B)), Q_LORA ** -0.5),
        'mla_w_ukv': normal((N_LAYERS_B, KV_LORA, H_B * (NOPE_B + V_B)), KV_LORA ** -0.5),
        'mla_qn_nope': gain((N_LAYERS_B, NOPE_B)),
        'mla_qn_rope': gain((N_LAYERS_B, ROPE_B)),
        'mla_kn_nope': gain((N_LAYERS_B, NOPE_B)),
        'mla_kn_rope': gain((N_LAYERS_B, ROPE_B)),
        'mla_w_out': normal((N_LAYERS_B, H_B * V_B, d), (H_B * V_B) ** -0.5),
        'gqa_w_in': normal((N_LAYERS_C, d, (H_C + 2 * KVH_C) * HD_C), d ** -0.5),
        'gqa_q_norm': gain((N_LAYERS_C, HD_C)),
        'gqa_k_norm': gain((N_LAYERS_C, HD_C)),
        'gqa_w_out': normal((N_LAYERS_C, H_C * HD_C, d), (H_C * HD_C) ** -0.5),
    }


def reference(x_prompt, x_sample, state_gdn_fwd, state_gdn_bwd, cache_mla_ckv, cache_mla_krope, cache_gqa_k,
              cache_gqa_v, c, c_ctx, norm_mix, norm_mlp, w_mod, b_mod, w_mlp_in, w_mlp_out, gdn_w_in, gdn_conv,
              gdn_a_log, gdn_dt_bias, gdn_out_norm, gdn_w_out, mla_w_down, mla_q_lat_norm, mla_kv_lat_norm,
              mla_w_uq, mla_w_ukv, mla_qn_nope, mla_qn_rope, mla_kn_nope, mla_kn_rope, mla_w_out, gqa_w_in,
              gqa_q_norm, gqa_k_norm, gqa_w_out):
    rows = x_sample.shape[1] // GRID_W
    row_pos, col_pos = grid_positions(rows)
    xp, xs = x_prompt, x_sample
    bp = x_prompt.shape[0]
    cond_ctx = c_ctx[None, :]
    gdn_f, gdn_b, mla_c, mla_r, gqa_k, gqa_v = [], [], [], [], [], []
    for i in range(DEPTH):
        kind, j = i % N_MIXERS, i // N_MIXERS
        mp = adaln(cond_ctx, w_mod[i], b_mod[i])
        ms = adaln(c, w_mod[i], b_mod[i])
        hp = modulated_norm(xp, norm_mix[i], mp[0], mp[1])
        hs = modulated_norm(xs, norm_mix[i], ms[0], ms[1])
        if kind == 0:
            p = (gdn_w_in[j], gdn_conv[j], gdn_a_log[j], gdn_dt_bias[j], gdn_out_norm[j], gdn_w_out[j])
            zero = jnp.zeros((bp, H_A, DK_A, DV_A), jnp.float32)
            op, s_f, s_b = gdn_mixer(hp, p, zero, zero)
            os_, _, _ = gdn_mixer(hs, p, state_gdn_fwd[:, j], state_gdn_bwd[:, j])
            gdn_f.append(s_f)
            gdn_b.append(s_b)
        elif kind == 1:
            p = (mla_w_down[j], mla_q_lat_norm[j], mla_kv_lat_norm[j], mla_w_uq[j], mla_w_ukv[j],
                 mla_qn_nope[j], mla_qn_rope[j], mla_kn_nope[j], mla_kn_rope[j], mla_w_out[j])
            op, ckv, kr = mla_context(hp, p)
            os_ = mla_latent(hs, cache_mla_ckv[:, j], cache_mla_krope[:, j], row_pos, col_pos, p)
            mla_c.append(ckv)
            mla_r.append(kr)
        else:
            p = (gqa_w_in[j], gqa_q_norm[j], gqa_k_norm[j], gqa_w_out[j])
            op, kc, vc = gqa_context(hp, p)
            os_ = gqa_latent(hs, cache_gqa_k[:, j], cache_gqa_v[:, j], row_pos, col_pos, p)
            gqa_k.append(kc)
            gqa_v.append(vc)
        xp = xp + mp[2] * op
        xs = xs + ms[2] * os_
        hp = modulated_norm(xp, norm_mlp[i], mp[3], mp[4])
        hs = modulated_norm(xs, norm_mlp[i], ms[3], ms[4])
        xp = xp + mp[5] * sq_relu_mlp(hp, w_mlp_in[i], w_mlp_out[i])
        xs = xs + ms[5] * sq_relu_mlp(hs, w_mlp_in[i], w_mlp_out[i])
    dt = x_prompt.dtype
    new_gdn_fwd = jnp.stack(gdn_f, axis=1).astype(dt)
    new_gdn_bwd = jnp.stack(gdn_b, axis=1).astype(dt)
    new_mla_ckv = jnp.stack(mla_c, axis=1).astype(dt)
    new_mla_krope = jnp.stack(mla_r, axis=1).astype(dt)
    new_gqa_k = jnp.stack(gqa_k, axis=1).astype(dt)
    new_gqa_v = jnp.stack(gqa_v, axis=1).astype(dt)
    return (xp, xs, new_gdn_fwd, new_gdn_bwd, new_mla_ckv, new_mla_krope, new_gqa_k, new_gqa_v)
```

```python
import functools
import math

import jax
import jax.numpy as jnp
from jax import lax
from jax.experimental import pallas as pl
from jax.experimental.pallas import tpu as pltpu

F32 = jnp.float32
BF16 = jnp.bfloat16

D_MODEL = 1024
DEPTH = 4
GRID_W = 64
N_MIXERS = 3
D_FF = 4 * D_MODEL
N_MOD = 6
EPS = 1e-6
ROPE_THETA = 10000.0
H_A = 8
DK_A = 128
DV_A = 128
WK_A = H_A * DK_A
WV_A = H_A * DV_A
CHUNK = 64
H_B = 8
Q_LORA = 384
KV_LORA = 256
NOPE_B = 128
ROPE_B = 64
V_B = 128
SLOT_B = 256
DOWN_PAD = 768
H_C = 8
KVH_C = 2
HD_C = 128

LANE = 128
MOD_ROWS = 16
CTX_ROW = 8
MOD_PAD = 8
VMEM_LIMIT = 56 * 1024 * 1024


def _cparams(*sem):
    return pltpu.CompilerParams(dimension_semantics=sem, vmem_limit_bytes=VMEM_LIMIT)


def _rms(x, n):
    return x * lax.rsqrt(jnp.sum(x * x, axis=-1, keepdims=True) * (1.0 / n) + EPS)


def _dot(a, b):
    return jnp.dot(a, b, preferred_element_type=F32)


def _dot_hi(a, b):
    return jnp.dot(a, b, preferred_element_type=F32, precision=lax.Precision.HIGHEST)


def _dot_nt(a, b):
    return lax.dot_general(a, b, (((1,), (1,)), ((), ())), preferred_element_type=F32)


def _dot_tn(a, b):
    return lax.dot_general(a, b, (((0,), (0,)), ((), ())), preferred_element_type=F32)


def _mod_index(rows_per_batch, tm):
    if rows_per_batch is None:
        return lambda t, *_: (CTX_ROW, 0, 0)
    return lambda t, *_: ((t * tm) // rows_per_batch, 0, 0)


def _adaln_kernel(c_ref, w_ref, b_ref, o_ref):
    c = c_ref[...]
    h = (c * jax.nn.sigmoid(c)).astype(BF16)
    o_ref[0] = _dot(h, w_ref[0].astype(BF16)) + b_ref[0]


def _adaln(cond, w_mod, b_mod):
    tn = 1024
    n = N_MOD * D_MODEL
    return pl.pallas_call(
        _adaln_kernel,
        grid=(DEPTH, n // tn),
        in_specs=[pl.BlockSpec((MOD_ROWS, D_MODEL), lambda i, j: (0, 0)),
                  pl.BlockSpec((1, D_MODEL, tn), lambda i, j: (i, 0, j)),
                  pl.BlockSpec((1, 1, tn), lambda i, j: (i, 0, j))],
        out_specs=pl.BlockSpec((1, MOD_ROWS, tn), lambda i, j: (i, 0, j)),
        out_shape=jax.ShapeDtypeStruct((DEPTH, MOD_ROWS, n), F32),
        compiler_params=_cparams("parallel", "parallel"),
        name="adaln",
    )(cond, w_mod, b_mod.reshape(DEPTH, 1, n))


def _modnorm(x, g, mod, shift_row, scale_row):
    y = _rms(x, D_MODEL) * g
    return y * (1.0 + mod[scale_row:scale_row + 1, :]) + mod[shift_row:shift_row + 1, :]


def _normproj_kernel(x_ref, mod_ref, g_ref, w_ref, o_ref, h_ref):
    @pl.when(pl.program_id(1) == 0)
    def _():
        h_ref[...] = _modnorm(x_ref[...], g_ref[...], mod_ref[0], 0, 1).astype(BF16)

    o_ref[...] = _dot(h_ref[...], w_ref[...])


def _normproj(x, mod, g, w, rows_per_batch, tm, tn):
    t, n = x.shape[0], w.shape[1]
    return pl.pallas_call(
        _normproj_kernel,
        grid=(t // tm, n // tn),
        in_specs=[pl.BlockSpec((tm, D_MODEL), lambda i, j: (i, 0)),
                  pl.BlockSpec((1, MOD_PAD, D_MODEL), _mod_index(rows_per_batch, tm)),
                  pl.BlockSpec((1, D_MODEL), lambda i, j: (0, 0)),
                  pl.BlockSpec((D_MODEL, tn), lambda i, j: (0, j))],
        out_specs=pl.BlockSpec((tm, tn), lambda i, j: (i, j)),
        out_shape=jax.ShapeDtypeStruct((t, n), F32),
        scratch_shapes=[pltpu.VMEM((tm, D_MODEL), BF16)],
        compiler_params=_cparams("parallel", "arbitrary"),
        name="normproj",
    )(x, mod, g, w)


def _outproj_kernel(x_ref, a_ref, mod_ref, w_ref, o_ref):
    o_ref[...] = x_ref[...] + mod_ref[0, 2:3, :] * _dot(a_ref[...], w_ref[...])


def _outproj(x, a, mod, w, rows_per_batch, tm):
    t = x.shape[0]
    return pl.pallas_call(
        _outproj_kernel,
        grid=(t // tm,),
        in_specs=[pl.BlockSpec((tm, D_MODEL), lambda i: (i, 0)),
                  pl.BlockSpec((tm, D_MODEL), lambda i: (i, 0)),
                  pl.BlockSpec((1, MOD_PAD, D_MODEL), _mod_index(rows_per_batch, tm)),
                  pl.BlockSpec((D_MODEL, D_MODEL), lambda i: (0, 0))],
        out_specs=pl.BlockSpec((tm, D_MODEL), lambda i: (i, 0)),
        out_shape=jax.ShapeDtypeStruct((t, D_MODEL), F32),
        compiler_params=_cparams("parallel"),
        name="outproj",
    )(x, a, mod, w)


def _mlp_kernel(x_ref, mod_ref, g_ref, win_ref, wout_ref, o_ref, h_ref, acc_ref):
    f = pl.program_id(1)

    @pl.when(f == 0)
    def _():
        h_ref[...] = _modnorm(x_ref[...], g_ref[...], mod_ref[0], 3, 4).astype(BF16)
        acc_ref[...] = jnp.zeros_like(acc_ref)

    a = jnp.maximum(_dot(h_ref[...], win_ref[...]), 0.0)
    acc_ref[...] += _dot((a * a).astype(BF16), wout_ref[...])

    @pl.when(f == pl.num_programs(1) - 1)
    def _():
        o_ref[...] = x_ref[...] + mod_ref[0, 5:6, :] * acc_ref[...]


def _mlp(x, mod, g, w_in, w_out, rows_per_batch, tm, tf):
    t = x.shape[0]
    return pl.pallas_call(
        _mlp_kernel,
        grid=(t // tm, D_FF // tf),
        in_specs=[pl.BlockSpec((tm, D_MODEL), lambda i, f: (i, 0)),
                  pl.BlockSpec((1, MOD_PAD, D_MODEL), _mod_index(rows_per_batch, tm)),
                  pl.BlockSpec((1, D_MODEL), lambda i, f: (0, 0)),
                  pl.BlockSpec((D_MODEL, tf), lambda i, f: (0, f)),
                  pl.BlockSpec((tf, D_MODEL), lambda i, f: (f, 0))],
        out_specs=pl.BlockSpec((tm, D_MODEL), lambda i, f: (i, 0)),
        out_shape=jax.ShapeDtypeStruct((t, D_MODEL), F32),
        scratch_shapes=[pltpu.VMEM((tm, D_MODEL), BF16), pltpu.VMEM((tm, D_MODEL), F32)],
        compiler_params=_cparams("parallel", "arbitrary"),
        name="mlp",
    )(x, mod, g, w_in, w_out)


def _attn_kernel(*refs, group, dk, dv, scale, has_ctx):
    if has_ctx:
        q_ref, k_ref, v_ref, kc_ref, vc_ref, o_ref = refs
    else:
        q_ref, k_ref, v_ref, o_ref = refs
    k = k_ref[...]
    v = v_ref[...]
    for g in range(group):
        q = q_ref[:, g * dk:(g + 1) * dk]
        s = _dot_nt(q, k) * scale
        m = jnp.max(s, axis=-1, keepdims=True)
        if has_ctx:
            sc = _dot_nt(q, kc_ref[...]) * scale
            m = jnp.maximum(m, jnp.max(sc, axis=-1, keepdims=True))
            pc = jnp.exp(sc - m)
        p = jnp.exp(s - m)
        l = jnp.sum(p, axis=-1, keepdims=True)
        o = _dot(p.astype(BF16), v)
        if has_ctx:
            l = l + jnp.sum(pc, axis=-1, keepdims=True)
            o = o + _dot(pc.astype(BF16), vc_ref[...])
        o_ref[:, g * dv:(g + 1) * dv] = (o / l).astype(o_ref.dtype)


def _attention(q, k, v, ctx, *, batch, seq, heads_kv, group, dk, dv, scale, tq):
    nq = seq // tq
    in_specs = [pl.BlockSpec((tq, group * dk), lambda b, j, i: (b * nq + i, j)),
                pl.BlockSpec((seq, dk), lambda b, j, i: (b, j)),
                pl.BlockSpec((seq, dv), lambda b, j, i: (b, j))]
    args = [q, k, v]
    if ctx is not None:
        past = ctx[0].shape[0] // batch
        in_specs += [pl.BlockSpec((past, dk), lambda b, j, i: (b, j)),
                     pl.BlockSpec((past, dv), lambda b, j, i: (b, j))]
        args += list(ctx)
    return pl.pallas_call(
        functools.partial(_attn_kernel, group=group, dk=dk, dv=dv, scale=scale, has_ctx=ctx is not None),
        grid=(batch, heads_kv, nq),
        in_specs=in_specs,
        out_specs=pl.BlockSpec((tq, group * dv), lambda b, j, i: (b * nq + i, j)),
        out_shape=jax.ShapeDtypeStruct((batch * seq, heads_kv * group * dv), BF16),
        compiler_params=_cparams("parallel", "parallel", "parallel"),
        name="attention",
    )(*args)


def _swap_quarters(x, quarter):
    lanes = x.shape[-1]
    lane = lax.broadcasted_iota(jnp.int32, x.shape, x.ndim - 1)
    up = pltpu.roll(x, lanes - quarter, x.ndim - 1)
    down = pltpu.roll(x, quarter, x.ndim - 1)
    return jnp.where((lane // quarter) % 2 == 0, up, down)


def _rope(x, cos, sin, quarter):
    return x * cos + _swap_quarters(x, quarter) * sin


def _rope_tables(seq, width, pad_to):
    rows = seq // GRID_W
    row = jnp.repeat(jnp.arange(rows, dtype=F32), GRID_W)
    col = jnp.tile(jnp.arange(GRID_W, dtype=F32), rows)
    quarter = width // 4
    freqs = ROPE_THETA ** (-jnp.arange(quarter, dtype=F32) / quarter)
    ar = row[:, None] * freqs[None, :]
    ac = col[:, None] * freqs[None, :]
    cos = jnp.concatenate([jnp.cos(ar), jnp.cos(ar), jnp.cos(ac), jnp.cos(ac)], axis=-1)
    sin = jnp.concatenate([-jnp.sin(ar), jnp.sin(ar), -jnp.sin(ac), jnp.sin(ac)], axis=-1)
    pad = ((0, 0), (0, pad_to - width))
    return jnp.pad(cos, pad), jnp.pad(sin, pad)


def _gqa_prep_kernel(*refs, use_rope):
    if use_rope:
        p_ref, qg_ref, kg_ref, cos_ref, sin_ref, q_ref, k_ref, v_ref, kf_ref, vf_ref = refs
    else:
        p_ref, qg_ref, kg_ref, q_ref, k_ref, v_ref, kf_ref, vf_ref = refs
    for h in range(H_C + KVH_C):
        x = _rms(p_ref[:, h * HD_C:(h + 1) * HD_C], HD_C)
        x = x * (qg_ref[...] if h < H_C else kg_ref[...])
        if h >= H_C:
            kf_ref[:, (h - H_C) * HD_C:(h - H_C + 1) * HD_C] = x
        if use_rope:
            x = _rope(x, cos_ref[...], sin_ref[...], HD_C // 4)
        if h < H_C:
            q_ref[:, h * HD_C:(h + 1) * HD_C] = x.astype(BF16)
        else:
            k_ref[:, (h - H_C) * HD_C:(h - H_C + 1) * HD_C] = x.astype(BF16)
    vv = p_ref[:, (H_C + KVH_C) * HD_C:]
    vf_ref[...] = vv
    v_ref[...] = vv.astype(BF16)


def _gqa_prep(proj, q_g, k_g, tables, seq, tm):
    t = proj.shape[0]
    nq, nkv = H_C * HD_C, KVH_C * HD_C
    in_specs = [pl.BlockSpec((tm, nq + 2 * nkv), lambda i: (i, 0)),
                pl.BlockSpec((1, HD_C), lambda i: (0, 0)),
                pl.BlockSpec((1, HD_C), lambda i: (0, 0))]
    args = [proj, q_g, k_g]
    if tables is not None:
        per = seq // tm
        in_specs += [pl.BlockSpec((tm, HD_C), lambda i: (i % per, 0))] * 2
        args += list(tables)
    return pl.pallas_call(
        functools.partial(_gqa_prep_kernel, use_rope=tables is not None),
        grid=(t // tm,),
        in_specs=in_specs,
        out_specs=[pl.BlockSpec((tm, nq), lambda i: (i, 0))] + [pl.BlockSpec((tm, nkv), lambda i: (i, 0))] * 4,
        out_shape=[jax.ShapeDtypeStruct((t, nq), BF16), jax.ShapeDtypeStruct((t, nkv), BF16),
                   jax.ShapeDtypeStruct((t, nkv), BF16), jax.ShapeDtypeStruct((t, nkv), F32),
                   jax.ShapeDtypeStruct((t, nkv), F32)],
        compiler_params=_cparams("parallel"),
        name="gqa_prep",
    )(*args)


def _mla_keys_values(ckv, krope_slot, wukv_ref, knn_ref, k_ref, v_ref):
    kv = _dot(ckv.astype(BF16), wukv_ref[...])
    kr = krope_slot.astype(BF16)
    for h in range(H_B):
        base = h * (NOPE_B + V_B)
        kn = _rms(kv[:, base:base + NOPE_B], NOPE_B) * knn_ref[...]
        k_ref[:, h * SLOT_B:h * SLOT_B + NOPE_B] = kn.astype(BF16)
        k_ref[:, h * SLOT_B + NOPE_B:(h + 1) * SLOT_B] = kr
        v_ref[:, h * V_B:(h + 1) * V_B] = kv[:, base + NOPE_B:base + NOPE_B + V_B].astype(BF16)


def _mla_prep_kernel(*refs, use_rope):
    if use_rope:
        (p_ref, qlg_ref, kvg_ref, krg_ref, wuq_ref, wukv_ref, qnn_ref, qnr_ref, knn_ref, cos_ref, sin_ref,
         q_ref, k_ref, v_ref, ckv_ref, kr_ref) = refs
    else:
        (p_ref, qlg_ref, kvg_ref, krg_ref, wuq_ref, wukv_ref, qnn_ref, qnr_ref, knn_ref,
         q_ref, k_ref, v_ref, ckv_ref, kr_ref) = refs
    cq = _rms(p_ref[:, :Q_LORA], Q_LORA) * qlg_ref[...]
    ckv = _rms(p_ref[:, Q_LORA:Q_LORA + KV_LORA], KV_LORA) * kvg_ref[...]
    kr = _rms(p_ref[:, Q_LORA + KV_LORA:], ROPE_B) * krg_ref[...]
    ckv_ref[...] = ckv
    kr_ref[...] = kr
    q = _dot(cq.astype(BF16), wuq_ref[...])
    for h in range(H_B):
        qn = _rms(q[:, h * SLOT_B:h * SLOT_B + NOPE_B], NOPE_B) * qnn_ref[...]
        qr = _rms(q[:, h * SLOT_B + NOPE_B:(h + 1) * SLOT_B], ROPE_B) * qnr_ref[...]
        if use_rope:
            qr = _rope(qr, cos_ref[...], sin_ref[...], ROPE_B // 4)
        q_ref[:, h * SLOT_B:h * SLOT_B + NOPE_B] = qn.astype(BF16)
        q_ref[:, h * SLOT_B + NOPE_B:(h + 1) * SLOT_B] = qr.astype(BF16)
    if use_rope:
        kr = _rope(kr, cos_ref[...], sin_ref[...], ROPE_B // 4)
    _mla_keys_values(ckv, kr, wukv_ref, knn_ref, k_ref, v_ref)


def _mla_prep(proj, p, tables, seq, tm):
    t = proj.shape[0]
    full = lambda shape: pl.BlockSpec(shape, lambda i: (0, 0))
    in_specs = [pl.BlockSpec((tm, DOWN_PAD), lambda i: (i, 0)),
                full((1, Q_LORA)), full((1, KV_LORA)), full((1, LANE)),
                full((Q_LORA, H_B * SLOT_B)), full((KV_LORA, H_B * (NOPE_B + V_B))),
                full((1, NOPE_B)), full((1, LANE)), full((1, NOPE_B))]
    args = [proj, p["q_lat_g"], p["kv_lat_g"], p["kn_rope"], p["w_uq"], p["w_ukv"], p["qn_nope"], p["qn_rope"],
            p["kn_nope"]]
    if tables is not None:
        per = seq // tm
        in_specs += [pl.BlockSpec((tm, LANE), lambda i: (i % per, 0))] * 2
        args += list(tables)
    row = lambda n: pl.BlockSpec((tm, n), lambda i: (i, 0))
    return pl.pallas_call(
        functools.partial(_mla_prep_kernel, use_rope=tables is not None),
        grid=(t // tm,),
        in_specs=in_specs,
        out_specs=[row(H_B * SLOT_B), row(H_B * SLOT_B), row(H_B * V_B), row(KV_LORA), row(LANE)],
        out_shape=[jax.ShapeDtypeStruct((t, H_B * SLOT_B), BF16), jax.ShapeDtypeStruct((t, H_B * SLOT_B), BF16),
                   jax.ShapeDtypeStruct((t, H_B * V_B), BF16), jax.ShapeDtypeStruct((t, KV_LORA), F32),
                   jax.ShapeDtypeStruct((t, LANE), F32)],
        compiler_params=_cparams("parallel"),
        name="mla_prep",
    )(*args)


def _mla_ctx_kernel(ckv_ref, kr_ref, wukv_ref, knn_ref, k_ref, v_ref):
    _mla_keys_values(ckv_ref[...], kr_ref[...], wukv_ref, knn_ref, k_ref, v_ref)


def _mla_ctx(ckv, krope_slot, p, tm):
    t = ckv.shape[0]
    full = lambda shape: pl.BlockSpec(shape, lambda i: (0, 0))
    row = lambda n: pl.BlockSpec((tm, n), lambda i: (i, 0))
    return pl.pallas_call(
        _mla_ctx_kernel,
        grid=(t // tm,),
        in_specs=[row(KV_LORA), row(LANE), full((KV_LORA, H_B * (NOPE_B + V_B))), full((1, NOPE_B))],
        out_specs=[row(H_B * SLOT_B), row(H_B * V_B)],
        out_shape=[jax.ShapeDtypeStruct((t, H_B * SLOT_B), BF16), jax.ShapeDtypeStruct((t, H_B * V_B), BF16)],
        compiler_params=_cparams("parallel"),
        name="mla_ctx",
    )(ckv, krope_slot, p["w_ukv"], p["kn_nope"])


def _conv_silu(x, w):
    s = x.shape[0]
    row = lax.broadcasted_iota(jnp.int32, x.shape, 0)
    prev = jnp.where(row == 0, 0.0, pltpu.roll(x, 1, 0))
    nxt = jnp.where(row == s - 1, 0.0, pltpu.roll(x, s - 1, 0))
    y = prev * w[0:1, :] + x * w[1:2, :] + nxt * w[2:3, :]
    return y * jax.nn.sigmoid(y)


def _l2n(x):
    return x * lax.rsqrt(jnp.sum(x * x, axis=-1, keepdims=True) + EPS)


def _softplus(x):
    return jnp.maximum(x, 0.0) + jnp.log1p(jnp.exp(-jnp.abs(x)))


def _gdn_kernel(alog_ref, dtb_ref, q_ref, k_ref, v_ref, z_ref, cq_ref, ck_ref, cv_ref, gate_ref,
                s0f_ref, s0b_ref, on_ref, o_ref, sf_ref, sb_ref, qs, ks, vs, oacc, st, *, seq):
    h = pl.program_id(1)
    n_chunks = seq // CHUNK
    qs[...] = _l2n(_conv_silu(q_ref[...], cq_ref[...])) * (DK_A ** -0.5)
    ks[...] = _l2n(_conv_silu(k_ref[...], ck_ref[...]))
    vs[...] = _conv_silu(v_ref[...], cv_ref[...])

    ii = lax.broadcasted_iota(jnp.int32, (CHUNK, CHUNK), 0)
    jj = lax.broadcasted_iota(jnp.int32, (CHUNK, CHUNK), 1)
    eye = ii == jj

    def to_col(r):
        return jnp.sum(jnp.where(eye, jnp.broadcast_to(r, (CHUNK, CHUNK)), 0.0), axis=1, keepdims=True)

    def to_row(c):
        return jnp.sum(jnp.where(eye, jnp.broadcast_to(c, (CHUNK, CHUNK)), 0.0), axis=0, keepdims=True)

    for d in range(2):
        incl = (ii >= jj) if d == 0 else (ii <= jj)
        strict = (ii > jj) if d == 0 else (ii < jj)
        last = CHUNK - 1 if d == 0 else 0
        neg_a = -jnp.exp(jnp.full((1, CHUNK), alog_ref[d, h], F32))
        dtb = dtb_ref[d, h]
        st[...] = (s0f_ref if d == 0 else s0b_ref)[0, 0]

        def chunk_step(ci, carry, d=d, incl=incl, strict=strict, last=last, neg_a=neg_a, dtb=dtb):
            c = ci if d == 0 else n_chunks - 1 - ci
            off = pl.multiple_of(c * CHUNK, CHUNK)
            qc = qs[pl.ds(off, CHUNK), :]
            kc = ks[pl.ds(off, CHUNK), :]
            vc = vs[pl.ds(off, CHUNK), :]
            g_row = neg_a * _softplus(gate_ref[0, d * H_A + h, pl.ds(c, 1), :] + dtb)
            beta_col = to_col(jax.nn.sigmoid(gate_ref[0, 2 * H_A + d * H_A + h, pl.ds(c, 1), :]))
            gcum_col = jnp.sum(jnp.where(incl, jnp.broadcast_to(g_row, (CHUNK, CHUNK)), 0.0), axis=1,
                               keepdims=True)
            gcum_row = to_row(gcum_col)
            g_tot = gcum_col[last:last + 1, :]
            decay = jnp.exp(jnp.where(incl, gcum_col - gcum_row, -jnp.inf))
            kb = kc * beta_col
            kcb = kc.astype(BF16)
            a_mat = jnp.where(strict, _dot_nt(kb.astype(BF16), kcb) * decay, 0.0)
            qk = _dot_nt(qc.astype(BF16), kcb) * decay
            bp = -a_mat
            nm = bp
            for _ in range(5):
                bp = _dot_hi(bp, bp)
                nm = nm + bp + _dot_hi(nm, bp)
            eg = jnp.exp(gcum_col)
            rhs = jnp.concatenate([vc * beta_col, kb * eg], axis=-1)
            sol = rhs + _dot_hi(nm, rhs)
            u, w = sol[:, :DV_A], sol[:, DV_A:]
            state = st[...]
            sb16 = state.astype(BF16)
            v_new = u - _dot(w.astype(BF16), sb16)
            o = _dot((qc * eg).astype(BF16), sb16) + _dot(qk.astype(BF16), v_new.astype(BF16))
            k_dec = kc * jnp.exp(g_tot - gcum_col)
            st[...] = state * jnp.exp(g_tot) + _dot_tn(k_dec.astype(BF16), v_new.astype(BF16))
            if d == 0:
                oacc[pl.ds(off, CHUNK), :] = o
            else:
                oacc[pl.ds(off, CHUNK), :] += o
            return carry

        lax.fori_loop(0, n_chunks, chunk_step, 0)
        (sf_ref if d == 0 else sb_ref)[0, 0] = st[...]

    z = z_ref[...]
    o_ref[...] = (_rms(oacc[...], DV_A) * on_ref[...] * (z * jax.nn.sigmoid(z))).astype(BF16)


def _gdn(proj, gates, conv_w, a_log, dt_bias, out_norm, s0f, s0b, batch, seq):
    n_chunks = seq // CHUNK
    smem = pl.BlockSpec(memory_space=pltpu.SMEM)
    col = lambda off: pl.BlockSpec((seq, DK_A), lambda b, h, off=off: (b, off + h))
    cw = lambda off: pl.BlockSpec((3, DK_A), lambda b, h, off=off: (0, off + h))
    state = pl.BlockSpec((1, 1, DK_A, DV_A), lambda b, h: (b, h, 0, 0))
    st_shape = jax.ShapeDtypeStruct((batch, H_A, DK_A, DV_A), F32)
    return pl.pallas_call(
        functools.partial(_gdn_kernel, seq=seq),
        grid=(batch, H_A),
        in_specs=[smem, smem, col(0), col(H_A), col(2 * H_A), col(3 * H_A), cw(0), cw(H_A), cw(2 * H_A),
                  pl.BlockSpec((1, 4 * H_A, n_chunks, CHUNK), lambda b, h: (b, 0, 0, 0)),
                  state, state, pl.BlockSpec((1, DV_A), lambda b, h: (0, 0))],
        out_specs=[pl.BlockSpec((seq, DV_A), lambda b, h: (b, h)), state, state],
        out_shape=[jax.ShapeDtypeStruct((batch * seq, WV_A), BF16), st_shape, st_shape],
        scratch_shapes=[pltpu.VMEM((seq, DK_A), F32), pltpu.VMEM((seq, DK_A), F32), pltpu.VMEM((seq, DV_A), F32),
                        pltpu.VMEM((seq, DV_A), F32), pltpu.VMEM((DK_A, DV_A), F32)],
        compiler_params=_cparams("parallel", "parallel"),
        name="gdn",
    )(a_log, dt_bias, proj, proj, proj, proj, conv_w, conv_w, conv_w, gates, s0f, s0b, out_norm)


def _pad_cols(w, n):
    return jnp.pad(w, ((0, 0), (0, n - w.shape[1])))


def kernel(x_prompt, x_sample, state_gdn_fwd, state_gdn_bwd, cache_mla_ckv, cache_mla_krope, cache_gqa_k, cache_gqa_v, c, c_ctx, norm_mix, norm_mlp, w_mod, b_mod, w_mlp_in, w_mlp_out, gdn_w_in, gdn_conv, gdn_a_log, gdn_dt_bias, gdn_out_norm, gdn_w_out, mla_w_down, mla_q_lat_norm, mla_kv_lat_norm, mla_w_uq, mla_w_ukv, mla_qn_nope, mla_qn_rope, mla_kn_nope, mla_kn_rope, mla_w_out, gqa_w_in, gqa_q_norm, gqa_k_norm, gqa_w_out):
    bp, sp, d = x_prompt.shape
    bs, ss, _ = x_sample.shape
    past = cache_mla_ckv.shape[2]
    xs = [x_prompt.reshape(bp * sp, d), x_sample.reshape(bs * ss, d)]
    batches, seqs = (bp, bs), (sp, ss)
    rows_per_batch = (None, ss)
    tms = (512, 1024)

    cond = jnp.zeros((MOD_ROWS, d), F32).at[:bs].set(c).at[CTX_ROW].set(c_ctx)
    mods = _adaln(cond, w_mod, b_mod).reshape(DEPTH, MOD_ROWS, N_MOD, d)
    mods = jnp.pad(mods, ((0, 0), (0, 0), (0, MOD_PAD - N_MOD), (0, 0)))

    gqa_tables = _rope_tables(ss, HD_C, HD_C)
    mla_tables = _rope_tables(ss, ROPE_B, LANE)

    new_gdn_f, new_gdn_b, new_ckv, new_kr, new_k, new_v = [], [], [], [], [], []
    for i in range(DEPTH):
        kind, j = i % N_MIXERS, i // N_MIXERS
        mod = mods[i]
        g_mix = norm_mix[i].reshape(1, d)
        g_mlp = norm_mlp[i].reshape(1, d)
        mixed = []
        if kind == 0:
            w_in = gdn_w_in[j]
            n_main = 2 * WK_A + 2 * WV_A
            w_main = w_in[:, :n_main].astype(BF16)
            w_gate = _pad_cols(w_in[:, n_main:], LANE).astype(BF16)
            w_o = gdn_w_out[j].astype(BF16)
            on = gdn_out_norm[j].reshape(1, DV_A)
            for gi in range(2):
                b, s, tm = batches[gi], seqs[gi], tms[gi]
                proj = _normproj(xs[gi], mod, g_mix, w_main, rows_per_batch[gi], tm, 512)
                gb = _normproj(xs[gi], mod, g_mix, w_gate, rows_per_batch[gi], tm, LANE)
                gates = gb[:, :4 * H_A].reshape(b, s // CHUNK, CHUNK, 4 * H_A).transpose(0, 3, 1, 2)
                if gi == 0:
                    s0f = s0b = jnp.zeros((b, H_A, DK_A, DV_A), F32)
                else:
                    s0f, s0b = state_gdn_fwd[:, j], state_gdn_bwd[:, j]
                o, s_f, s_b = _gdn(proj, gates, gdn_conv[j], gdn_a_log[j], gdn_dt_bias[j], on, s0f, s0b, b, s)
                if gi == 0:
                    new_gdn_f.append(s_f)
                    new_gdn_b.append(s_b)
                mixed.append((o, w_o))
        elif kind == 1:
            w_uq = mla_w_uq[j].reshape(Q_LORA, H_B, NOPE_B + ROPE_B)
            w_uq = jnp.pad(w_uq, ((0, 0), (0, 0), (0, SLOT_B - NOPE_B - ROPE_B))).reshape(Q_LORA, H_B * SLOT_B)
            p = dict(q_lat_g=mla_q_lat_norm[j].reshape(1, Q_LORA), kv_lat_g=mla_kv_lat_norm[j].reshape(1, KV_LORA),
                     kn_rope=_pad_cols(mla_kn_rope[j].reshape(1, ROPE_B), LANE),
                     qn_rope=_pad_cols(mla_qn_rope[j].reshape(1, ROPE_B), LANE),
                     qn_nope=mla_qn_nope[j].reshape(1, NOPE_B), kn_nope=mla_kn_nope[j].reshape(1, NOPE_B),
                     w_uq=w_uq.astype(BF16), w_ukv=mla_w_ukv[j].astype(BF16))
            w_down = _pad_cols(mla_w_down[j], DOWN_PAD).astype(BF16)
            w_o = mla_w_out[j].astype(BF16)
            scale = (NOPE_B + ROPE_B) ** -0.5
            for gi in range(2):
                b, s, tm = batches[gi], seqs[gi], tms[gi]
                proj = _normproj(xs[gi], mod, g_mix, w_down, rows_per_batch[gi], tm, DOWN_PAD)
                q, k, v, ckv, kr = _mla_prep(proj, p, mla_tables if gi == 1 else None, s, 512)
                ctx = None
                if gi == 0:
                    new_ckv.append(ckv.reshape(b, s, KV_LORA))
                    new_kr.append(kr[:, :ROPE_B].reshape(b, s, ROPE_B))
                else:
                    ckv_c = cache_mla_ckv[:, j].reshape(b * past, KV_LORA)
                    kr_c = _pad_cols(cache_mla_krope[:, j].reshape(b * past, ROPE_B), LANE)
                    ctx = _mla_ctx(ckv_c, kr_c, p, 512)
                o = _attention(q, k, v, ctx, batch=b, seq=s, heads_kv=H_B, group=1, dk=SLOT_B, dv=V_B,
                               scale=scale, tq=256)
                mixed.append((o, w_o))
        else:
            w_in = gqa_w_in[j].astype(BF16)
            w_o = gqa_w_out[j].astype(BF16)
            q_g, k_g = gqa_q_norm[j].reshape(1, HD_C), gqa_k_norm[j].reshape(1, HD_C)
            for gi in range(2):
                b, s, tm = batches[gi], seqs[gi], tms[gi]
                proj = _normproj(xs[gi], mod, g_mix, w_in, rows_per_batch[gi], tm, 512)
                q, k, v, kf, vf = _gqa_prep(proj, q_g, k_g, gqa_tables if gi == 1 else None, s, 512)
                ctx = None
                if gi == 0:
                    new_k.append(kf.reshape(b, s, KVH_C, HD_C))
                    new_v.append(vf.reshape(b, s, KVH_C, HD_C))
                else:
                    ctx = (cache_gqa_k[:, j].reshape(b * past, KVH_C * HD_C).astype(BF16),
                           cache_gqa_v[:, j].reshape(b * past, KVH_C * HD_C).astype(BF16))
                o = _attention(q, k, v, ctx, batch=b, seq=s, heads_kv=KVH_C, group=H_C // KVH_C, dk=HD_C,
                               dv=HD_C, scale=HD_C ** -0.5, tq=256)
                mixed.append((o, w_o))
        w_mi, w_mo = w_mlp_in[i].astype(BF16), w_mlp_out[i].astype(BF16)
        for gi in range(2):
            o, w_o = mixed[gi]
            x_mid = _outproj(xs[gi], o, mod, w_o, rows_per_batch[gi], tms[gi])
            xs[gi] = _mlp(x_mid, mod, g_mlp, w_mi, w_mo, rows_per_batch[gi], tms[gi], 512)

    dt = x_prompt.dtype
    stack = lambda lst: jnp.stack(lst, axis=1).astype(dt)
    return (xs[0].reshape(bp, sp, d), xs[1].reshape(bs, ss, d), stack(new_gdn_f), stack(new_gdn_b),
            stack(new_ckv), stack(new_kr), stack(new_k), stack(new_v))
```

```python
import functools
import math

import jax
import jax.numpy as jnp
from jax import lax
from jax.experimental import pallas as pl
from jax.experimental.pallas import tpu as pltpu

F32 = jnp.float32
BF16 = jnp.bfloat16

D_MODEL = 1024
DEPTH = 4
GRID_W = 64
N_MIXERS = 3
D_FF = 4 * D_MODEL
N_MOD = 6
EPS = 1e-6
ROPE_THETA = 10000.0
H_A = 8
DK_A = 128
DV_A = 128
WK_A = H_A * DK_A
WV_A = H_A * DV_A
CHUNK = 64
H_B = 8
Q_LORA = 384
KV_LORA = 256
NOPE_B = 128
ROPE_B = 64
V_B = 128
SLOT_B = 256
DOWN_PAD = 768
H_C = 8
KVH_C = 2
HD_C = 128

LANE = 128
MOD_ROWS = 16
CTX_ROW = 8
MOD_PAD = 8
VMEM_LIMIT = 56 * 1024 * 1024


def _cparams(*sem):
    return pltpu.CompilerParams(dimension_semantics=sem, vmem_limit_bytes=VMEM_LIMIT)


def _rms(x, n):
    return x * lax.rsqrt(jnp.sum(x * x, axis=-1, keepdims=True) * (1.0 / n) + EPS)


def _dot(a, b):
    return jnp.dot(a, b, preferred_element_type=F32)


def _dot_hi(a, b):
    return jnp.dot(a, b, preferred_element_type=F32, precision=lax.Precision.HIGHEST)


def _dot_nt(a, b):
    return lax.dot_general(a, b, (((1,), (1,)), ((), ())), preferred_element_type=F32)


def _dot_tn(a, b):
    return lax.dot_general(a, b, (((0,), (0,)), ((), ())), preferred_element_type=F32)


def _mod_index(rows_per_batch, tm):
    if rows_per_batch is None:
        return lambda t, *_: (CTX_ROW, 0, 0)
    return lambda t, *_: ((t * tm) // rows_per_batch, 0, 0)


def _adaln_kernel(c_ref, w_ref, b_ref, o_ref):
    c = c_ref[...]
    h = (c * jax.nn.sigmoid(c)).astype(BF16)
    o_ref[0] = _dot(h, w_ref[0].astype(BF16)) + b_ref[0]


def _adaln(cond, w_mod, b_mod):
    tn = 1024
    n = N_MOD * D_MODEL
    return pl.pallas_call(
        _adaln_kernel,
        grid=(DEPTH, n // tn),
        in_specs=[pl.BlockSpec((MOD_ROWS, D_MODEL), lambda i, j: (0, 0)),
                  pl.BlockSpec((1, D_MODEL, tn), lambda i, j: (i, 0, j)),
                  pl.BlockSpec((1, 1, tn), lambda i, j: (i, 0, j))],
        out_specs=pl.BlockSpec((1, MOD_ROWS, tn), lambda i, j: (i, 0, j)),
        out_shape=jax.ShapeDtypeStruct((DEPTH, MOD_ROWS, n), F32),
        compiler_params=_cparams("parallel", "parallel"),
        name="adaln",
    )(cond, w_mod, b_mod.reshape(DEPTH, 1, n))


def _modnorm(x, g, mod, shift_row, scale_row):
    y = _rms(x, D_MODEL) * g
    return y * (1.0 + mod[scale_row:scale_row + 1, :]) + mod[shift_row:shift_row + 1, :]


def _normproj_kernel(x_ref, mod_ref, g_ref, w_ref, o_ref, h_ref):
    @pl.when(pl.program_id(1) == 0)
    def _():
        h_ref[...] = _modnorm(x_ref[...], g_ref[...], mod_ref[0], 0, 1).astype(BF16)

    o_ref[...] = _dot(h_ref[...], w_ref[...])


def _normproj(x, mod, g, w, rows_per_batch, tm, tn):
    t, n = x.shape[0], w.shape[1]
    return pl.pallas_call(
        _normproj_kernel,
        grid=(t // tm, n // tn),
        in_specs=[pl.BlockSpec((tm, D_MODEL), lambda i, j: (i, 0)),
                  pl.BlockSpec((1, MOD_PAD, D_MODEL), _mod_index(rows_per_batch, tm)),
                  pl.BlockSpec((1, D_MODEL), lambda i, j: (0, 0)),
                  pl.BlockSpec((D_MODEL, tn), lambda i, j: (0, j))],
        out_specs=pl.BlockSpec((tm, tn), lambda i, j: (i, j)),
        out_shape=jax.ShapeDtypeStruct((t, n), F32),
        scratch_shapes=[pltpu.VMEM((tm, D_MODEL), BF16)],
        compiler_params=_cparams("parallel", "arbitrary"),
        name="normproj",
    )(x, mod, g, w)


def _outproj_kernel(x_ref, a_ref, mod_ref, w_ref, o_ref):
    o_ref[...] = x_ref[...] + mod_ref[0, 2:3, :] * _dot(a_ref[...], w_ref[...])


def _outproj(x, a, mod, w, rows_per_batch, tm):
    t = x.shape[0]
    return pl.pallas_call(
        _outproj_kernel,
        grid=(t // tm,),
        in_specs=[pl.BlockSpec((tm, D_MODEL), lambda i: (i, 0)),
                  pl.BlockSpec((tm, D_MODEL), lambda i: (i, 0)),
                  pl.BlockSpec((1, MOD_PAD, D_MODEL), _mod_index(rows_per_batch, tm)),
                  pl.BlockSpec((D_MODEL, D_MODEL), lambda i: (0, 0))],
        out_specs=pl.BlockSpec((tm, D_MODEL), lambda i: (i, 0)),
        out_shape=jax.ShapeDtypeStruct((t, D_MODEL), F32),
        compiler_params=_cparams("parallel"),
        name="outproj",
    )(x, a, mod, w)


def _mlp_kernel(x_ref, mod_ref, g_ref, win_ref, wout_ref, o_ref, h_ref, acc_ref):
    f = pl.program_id(1)

    @pl.when(f == 0)
    def _():
        h_ref[...] = _modnorm(x_ref[...], g_ref[...], mod_ref[0], 3, 4).astype(BF16)
        acc_ref[...] = jnp.zeros_like(acc_ref)

    a = jnp.maximum(_dot(h_ref[...], win_ref[...]), 0.0)
    acc_ref[...] += _dot((a * a).astype(BF16), wout_ref[...])

    @pl.when(f == pl.num_programs(1) - 1)
    def _():
        o_ref[...] = x_ref[...] + mod_ref[0, 5:6, :] * acc_ref[...]


def _mlp(x, mod, g, w_in, w_out, rows_per_batch, tm, tf):
    t = x.shape[0]
    return pl.pallas_call(
        _mlp_kernel,
        grid=(t // tm, D_FF // tf),
        in_specs=[pl.BlockSpec((tm, D_MODEL), lambda i, f: (i, 0)),
                  pl.BlockSpec((1, MOD_PAD, D_MODEL), _mod_index(rows_per_batch, tm)),
                  pl.BlockSpec((1, D_MODEL), lambda i, f: (0, 0)),
                  pl.BlockSpec((D_MODEL, tf), lambda i, f: (0, f)),
                  pl.BlockSpec((tf, D_MODEL), lambda i, f: (f, 0))],
        out_specs=pl.BlockSpec((tm, D_MODEL), lambda i, f: (i, 0)),
        out_shape=jax.ShapeDtypeStruct((t, D_MODEL), F32),
        scratch_shapes=[pltpu.VMEM((tm, D_MODEL), BF16), pltpu.VMEM((tm, D_MODEL), F32)],
        compiler_params=_cparams("parallel", "arbitrary"),
        name="mlp",
    )(x, mod, g, w_in, w_out)


def _attn_kernel(*refs, kv_per_step, group, dk, dv, has_ctx):
    if has_ctx:
        q_ref, k_ref, v_ref, kc_ref, vc_ref, o_ref = refs
    else:
        q_ref, k_ref, v_ref, o_ref = refs

    def scores(n):
        j = n // group
        q = q_ref[:, n * dk:(n + 1) * dk]
        s = _dot_nt(q, k_ref[:, j * dk:(j + 1) * dk])
        sc = _dot_nt(q, kc_ref[:, j * dk:(j + 1) * dk]) if has_ctx else None
        return s, sc

    heads = kv_per_step * group
    nxt = scores(0)
    for n in range(heads):
        s, sc = nxt
        if n + 1 < heads:
            nxt = scores(n + 1)
        j = n // group
        m = jnp.max(s, axis=-1, keepdims=True)
        if has_ctx:
            m = jnp.maximum(m, jnp.max(sc, axis=-1, keepdims=True))
            pc = jnp.exp(sc - m)
        p = jnp.exp(s - m)
        l = jnp.sum(p, axis=-1, keepdims=True)
        o = _dot(p.astype(BF16), v_ref[:, j * dv:(j + 1) * dv])
        if has_ctx:
            l = l + jnp.sum(pc, axis=-1, keepdims=True)
            o = o + _dot(pc.astype(BF16), vc_ref[:, j * dv:(j + 1) * dv])
        o_ref[:, n * dv:(n + 1) * dv] = (o / l).astype(o_ref.dtype)


def _attention(q, k, v, ctx, *, batch, seq, heads_kv, kv_per_step, group, dk, dv, tq):
    nq = seq // tq
    hp = kv_per_step
    in_specs = [pl.BlockSpec((tq, hp * group * dk), lambda b, j, i: (b * nq + i, j)),
                pl.BlockSpec((seq, hp * dk), lambda b, j, i: (b, j)),
                pl.BlockSpec((seq, hp * dv), lambda b, j, i: (b, j))]
    args = [q, k, v]
    if ctx is not None:
        past = ctx[0].shape[0] // batch
        in_specs += [pl.BlockSpec((past, hp * dk), lambda b, j, i: (b, j)),
                     pl.BlockSpec((past, hp * dv), lambda b, j, i: (b, j))]
        args += list(ctx)
    return pl.pallas_call(
        functools.partial(_attn_kernel, kv_per_step=hp, group=group, dk=dk, dv=dv, has_ctx=ctx is not None),
        grid=(batch, heads_kv // hp, nq),
        in_specs=in_specs,
        out_specs=pl.BlockSpec((tq, hp * group * dv), lambda b, j, i: (b * nq + i, j)),
        out_shape=jax.ShapeDtypeStruct((batch * seq, heads_kv * group * dv), BF16),
        compiler_params=_cparams("parallel", "parallel", "parallel"),
        name="attention",
    )(*args)


def _swap_quarters(x, quarter):
    lanes = x.shape[-1]
    lane = lax.broadcasted_iota(jnp.int32, x.shape, x.ndim - 1)
    up = pltpu.roll(x, lanes - quarter, x.ndim - 1)
    down = pltpu.roll(x, quarter, x.ndim - 1)
    return jnp.where((lane // quarter) % 2 == 0, up, down)


def _rope(x, cos, sin, quarter):
    return x * cos + _swap_quarters(x, quarter) * sin


def _rope_tables(seq, width, pad_to):
    rows = seq // GRID_W
    row = jnp.repeat(jnp.arange(rows, dtype=F32), GRID_W)
    col = jnp.tile(jnp.arange(GRID_W, dtype=F32), rows)
    quarter = width // 4
    freqs = ROPE_THETA ** (-jnp.arange(quarter, dtype=F32) / quarter)
    ar = row[:, None] * freqs[None, :]
    ac = col[:, None] * freqs[None, :]
    cos = jnp.concatenate([jnp.cos(ar), jnp.cos(ar), jnp.cos(ac), jnp.cos(ac)], axis=-1)
    sin = jnp.concatenate([-jnp.sin(ar), jnp.sin(ar), -jnp.sin(ac), jnp.sin(ac)], axis=-1)
    pad = ((0, 0), (0, pad_to - width))
    return jnp.pad(cos, pad), jnp.pad(sin, pad)


def _gqa_prep_kernel(*refs, use_rope):
    if use_rope:
        p_ref, qg_ref, kg_ref, cos_ref, sin_ref, q_ref, k_ref, v_ref, kf_ref, vf_ref = refs
    else:
        p_ref, qg_ref, kg_ref, q_ref, k_ref, v_ref, kf_ref, vf_ref = refs
    for h in range(H_C + KVH_C):
        x = _rms(p_ref[:, h * HD_C:(h + 1) * HD_C], HD_C)
        x = x * (qg_ref[...] if h < H_C else kg_ref[...])
        if h >= H_C:
            kf_ref[:, (h - H_C) * HD_C:(h - H_C + 1) * HD_C] = x
        if use_rope:
            x = _rope(x, cos_ref[...], sin_ref[...], HD_C // 4)
        if h < H_C:
            q_ref[:, h * HD_C:(h + 1) * HD_C] = (x * HD_C ** -0.5).astype(BF16)
        else:
            k_ref[:, (h - H_C) * HD_C:(h - H_C + 1) * HD_C] = x.astype(BF16)
    vv = p_ref[:, (H_C + KVH_C) * HD_C:]
    vf_ref[...] = vv
    v_ref[...] = vv.astype(BF16)


def _gqa_prep(proj, q_g, k_g, tables, seq, tm):
    t = proj.shape[0]
    nq, nkv = H_C * HD_C, KVH_C * HD_C
    in_specs = [pl.BlockSpec((tm, nq + 2 * nkv), lambda i: (i, 0)),
                pl.BlockSpec((1, HD_C), lambda i: (0, 0)),
                pl.BlockSpec((1, HD_C), lambda i: (0, 0))]
    args = [proj, q_g, k_g]
    if tables is not None:
        per = seq // tm
        in_specs += [pl.BlockSpec((tm, HD_C), lambda i: (i % per, 0))] * 2
        args += list(tables)
    return pl.pallas_call(
        functools.partial(_gqa_prep_kernel, use_rope=tables is not None),
        grid=(t // tm,),
        in_specs=in_specs,
        out_specs=[pl.BlockSpec((tm, nq), lambda i: (i, 0))] + [pl.BlockSpec((tm, nkv), lambda i: (i, 0))] * 4,
        out_shape=[jax.ShapeDtypeStruct((t, nq), BF16), jax.ShapeDtypeStruct((t, nkv), BF16),
                   jax.ShapeDtypeStruct((t, nkv), BF16), jax.ShapeDtypeStruct((t, nkv), F32),
                   jax.ShapeDtypeStruct((t, nkv), F32)],
        compiler_params=_cparams("parallel"),
        name="gqa_prep",
    )(*args)


def _mla_keys_values(ckv, krope_slot, wukv_ref, knn_ref, k_ref, v_ref):
    kv = _dot(ckv.astype(BF16), wukv_ref[...])
    kr = krope_slot.astype(BF16)
    for h in range(H_B):
        base = h * (NOPE_B + V_B)
        kn = _rms(kv[:, base:base + NOPE_B], NOPE_B) * knn_ref[...]
        k_ref[:, h * SLOT_B:h * SLOT_B + NOPE_B] = kn.astype(BF16)
        k_ref[:, h * SLOT_B + NOPE_B:(h + 1) * SLOT_B] = kr
        v_ref[:, h * V_B:(h + 1) * V_B] = kv[:, base + NOPE_B:base + NOPE_B + V_B].astype(BF16)


def _mla_prep_kernel(*refs, use_rope):
    if use_rope:
        (p_ref, qlg_ref, kvg_ref, krg_ref, wuq_ref, wukv_ref, qnn_ref, qnr_ref, knn_ref, cos_ref, sin_ref,
         q_ref, k_ref, v_ref, ckv_ref, kr_ref) = refs
    else:
        (p_ref, qlg_ref, kvg_ref, krg_ref, wuq_ref, wukv_ref, qnn_ref, qnr_ref, knn_ref,
         q_ref, k_ref, v_ref, ckv_ref, kr_ref) = refs
    cq = _rms(p_ref[:, :Q_LORA], Q_LORA) * qlg_ref[...]
    ckv = _rms(p_ref[:, Q_LORA:Q_LORA + KV_LORA], KV_LORA) * kvg_ref[...]
    kr = _rms(p_ref[:, Q_LORA + KV_LORA:], ROPE_B) * krg_ref[...]
    ckv_ref[...] = ckv
    kr_ref[...] = kr
    q = _dot(cq.astype(BF16), wuq_ref[...])
    scale = (NOPE_B + ROPE_B) ** -0.5
    for h in range(H_B):
        qn = _rms(q[:, h * SLOT_B:h * SLOT_B + NOPE_B], NOPE_B) * qnn_ref[...]
        qr = _rms(q[:, h * SLOT_B + NOPE_B:(h + 1) * SLOT_B], ROPE_B) * qnr_ref[...]
        if use_rope:
            qr = _rope(qr, cos_ref[...], sin_ref[...], ROPE_B // 4)
        q_ref[:, h * SLOT_B:h * SLOT_B + NOPE_B] = (qn * scale).astype(BF16)
        q_ref[:, h * SLOT_B + NOPE_B:(h + 1) * SLOT_B] = (qr * scale).astype(BF16)
    if use_rope:
        kr = _rope(kr, cos_ref[...], sin_ref[...], ROPE_B // 4)
    _mla_keys_values(ckv, kr, wukv_ref, knn_ref, k_ref, v_ref)


def _mla_prep(proj, p, tables, seq, tm):
    t = proj.shape[0]
    full = lambda shape: pl.BlockSpec(shape, lambda i: (0, 0))
    in_specs = [pl.BlockSpec((tm, DOWN_PAD), lambda i: (i, 0)),
                full((1, Q_LORA)), full((1, KV_LORA)), full((1, LANE)),
                full((Q_LORA, H_B * SLOT_B)), full((KV_LORA, H_B * (NOPE_B + V_B))),
                full((1, NOPE_B)), full((1, LANE)), full((1, NOPE_B))]
    args = [proj, p["q_lat_g"], p["kv_lat_g"], p["kn_rope"], p["w_uq"], p["w_ukv"], p["qn_nope"], p["qn_rope"],
            p["kn_nope"]]
    if tables is not None:
        per = seq // tm
        in_specs += [pl.BlockSpec((tm, LANE), lambda i: (i % per, 0))] * 2
        args += list(tables)
    row = lambda n: pl.BlockSpec((tm, n), lambda i: (i, 0))
    return pl.pallas_call(
        functools.partial(_mla_prep_kernel, use_rope=tables is not None),
        grid=(t // tm,),
        in_specs=in_specs,
        out_specs=[row(H_B * SLOT_B), row(H_B * SLOT_B), row(H_B * V_B), row(KV_LORA), row(LANE)],
        out_shape=[jax.ShapeDtypeStruct((t, H_B * SLOT_B), BF16), jax.ShapeDtypeStruct((t, H_B * SLOT_B), BF16),
                   jax.ShapeDtypeStruct((t, H_B * V_B), BF16), jax.ShapeDtypeStruct((t, KV_LORA), F32),
                   jax.ShapeDtypeStruct((t, LANE), F32)],
        compiler_params=_cparams("parallel"),
        name="mla_prep",
    )(*args)


def _mla_ctx_kernel(ckv_ref, kr_ref, wukv_ref, knn_ref, k_ref, v_ref):
    _mla_keys_values(ckv_ref[...], kr_ref[...], wukv_ref, knn_ref, k_ref, v_ref)


def _mla_ctx(ckv, krope_slot, p, tm):
    t = ckv.shape[0]
    full = lambda shape: pl.BlockSpec(shape, lambda i: (0, 0))
    row = lambda n: pl.BlockSpec((tm, n), lambda i: (i, 0))
    return pl.pallas_call(
        _mla_ctx_kernel,
        grid=(t // tm,),
        in_specs=[row(KV_LORA), row(LANE), full((KV_LORA, H_B * (NOPE_B + V_B))), full((1, NOPE_B))],
        out_specs=[row(H_B * SLOT_B), row(H_B * V_B)],
        out_shape=[jax.ShapeDtypeStruct((t, H_B * SLOT_B), BF16), jax.ShapeDtypeStruct((t, H_B * V_B), BF16)],
        compiler_params=_cparams("parallel"),
        name="mla_ctx",
    )(ckv, krope_slot, p["w_ukv"], p["kn_nope"])


def _conv_silu(x, w):
    s = x.shape[0]
    row = lax.broadcasted_iota(jnp.int32, x.shape, 0)
    prev = jnp.where(row == 0, 0.0, pltpu.roll(x, 1, 0))
    nxt = jnp.where(row == s - 1, 0.0, pltpu.roll(x, s - 1, 0))
    y = prev * w[0:1, :] + x * w[1:2, :] + nxt * w[2:3, :]
    return y * jax.nn.sigmoid(y)


def _l2n(x):
    return x * lax.rsqrt(jnp.sum(x * x, axis=-1, keepdims=True) + EPS)


def _softplus(x):
    return jnp.maximum(x, 0.0) + jnp.log1p(jnp.exp(-jnp.abs(x)))


GDN_MAX_UNROLL = 8


def _gdn_kernel(alog_ref, dtb_ref, q_ref, k_ref, v_ref, z_ref, cq_ref, ck_ref, cv_ref, gate_ref,
                s0f_ref, s0b_ref, on_ref, o_ref, sf_ref, sb_ref,
                qs, ks, vs, obuf, qeff, nmat, mneg, egt, st, *, seq, unroll):
    h = pl.program_id(1)
    n_chunks = seq // CHUNK
    qs[...] = _l2n(_conv_silu(q_ref[...], cq_ref[...])) * (DK_A ** -0.5)
    ks[...] = _l2n(_conv_silu(k_ref[...], ck_ref[...]))
    vs[...] = _conv_silu(v_ref[...], cv_ref[...])

    ii = lax.broadcasted_iota(jnp.int32, (CHUNK, CHUNK), 0)
    jj = lax.broadcasted_iota(jnp.int32, (CHUNK, CHUNK), 1)
    eye = ii == jj
    incl = (ii >= jj, ii <= jj)
    strict = (ii > jj, ii < jj)
    last = (CHUNK - 1, 0)
    pair = [(ii // 2) == (jj // 2)]
    pair += [((ii // (2 * s)) == (jj // (2 * s))) & ((ii // s) != (jj // s)) for s in (2, 4, 8, 16, 32)]
    neg_a = [-jnp.exp(jnp.full((1, CHUNK), alog_ref[d, h], F32)) for d in range(2)]
    dtb = [dtb_ref[d, h] for d in range(2)]

    def to_col(r):
        return jnp.sum(jnp.where(eye, jnp.broadcast_to(r, (CHUNK, CHUNK)), 0.0), axis=1, keepdims=True)

    def to_row(c):
        return jnp.sum(jnp.where(eye, jnp.broadcast_to(c, (CHUNK, CHUNK)), 0.0), axis=0, keepdims=True)

    def prepare(blk, carry):
        cs = [blk * unroll + u for u in range(unroll)]
        offs = [pl.multiple_of(c * CHUNK, CHUNK) for c in cs]
        qc = [qs[pl.ds(off, CHUNK), :] for off in offs]
        kc = [ks[pl.ds(off, CHUNK), :] for off in offs]
        vc = [vs[pl.ds(off, CHUNK), :] for off in offs]
        kcb = [x.astype(BF16) for x in kc]
        kk = [_dot_nt(x, x) for x in kcb]
        qk_raw = [_dot_nt(q.astype(BF16), k) for q, k in zip(qc, kcb)]
        chains = [(u, d) for u in range(unroll) for d in range(2)]
        beta_col, gcum_col, g_tot, eg, qk, a_mat = [], [], [], [], [], []
        for u, d in chains:
            c = cs[u]
            g_row = neg_a[d] * _softplus(gate_ref[0, d * H_A + h, pl.ds(c, 1), :] + dtb[d])
            beta_col.append(to_col(jax.nn.sigmoid(gate_ref[0, 2 * H_A + d * H_A + h, pl.ds(c, 1), :])))
            gcum_col.append(jnp.sum(jnp.where(incl[d], jnp.broadcast_to(g_row, (CHUNK, CHUNK)), 0.0), axis=1,
                                    keepdims=True))
            g_tot.append(gcum_col[-1][last[d]:last[d] + 1, :])
            decay = jnp.exp(jnp.where(incl[d], gcum_col[-1] - to_row(gcum_col[-1]), -jnp.inf))
            eg.append(jnp.exp(gcum_col[-1]))
            qk.append((qk_raw[u] * decay).astype(BF16))
            a_mat.append(jnp.where(strict[d], kk[u] * beta_col[-1] * decay, 0.0))
        tm = [jnp.where(eye, 1.0, -jnp.where(pair[0], a, 0.0)) for a in a_mat]
        for lvl in range(1, len(pair)):
            tb = [t.astype(BF16) for t in tm]
            x = [_dot(jnp.where(pair[lvl], a, 0.0).astype(BF16), t) for a, t in zip(a_mat, tb)]
            tm = [t - _dot(t16, xi.astype(BF16)) for t, t16, xi in zip(tm, tb, x)]
        nm = [jnp.where(eye, 0.0, t) for t in tm]
        rhs = [jnp.concatenate([vc[u] * beta_col[i], kc[u] * (beta_col[i] * eg[i])], axis=-1)
               for i, (u, d) in enumerate(chains)]
        sol = [(r + _dot(n.astype(BF16), r.astype(BF16))).astype(BF16) for n, r in zip(nm, rhs)]
        local = [_dot(a, b) for a, b in zip(qk, sol)]
        k_dec = [(kc[u] * jnp.exp(g_tot[i] - gcum_col[i])).astype(BF16) for i, (u, d) in enumerate(chains)]
        nm_mat = [_dot_tn(a, b) for a, b in zip(k_dec, sol)]
        for i, (u, d) in enumerate(chains):
            obuf[d, pl.ds(offs[u], CHUNK), :] = local[i][:, :DV_A]
            qeff[d, pl.ds(offs[u], CHUNK), :] = (qc[u] * eg[i] - local[i][:, DV_A:]).astype(BF16)
            nmat[d, cs[u]] = nm_mat[i][:, :DV_A]
            mneg[d, cs[u]] = (-nm_mat[i][:, DV_A:]).astype(BF16)
            egt[d, cs[u]] = jnp.broadcast_to(jnp.exp(g_tot[i]), (8, DV_A))
        return carry

    lax.fori_loop(0, n_chunks // unroll, prepare, 0)

    st[0] = s0f_ref[0, 0]
    st[1] = s0b_ref[0, 0]

    def scan(ci, carry):
        for d in range(2):
            c = ci if d == 0 else n_chunks - 1 - ci
            off = pl.multiple_of(c * CHUNK, CHUNK)
            state = st[d]
            sb16 = state.astype(BF16)
            obuf[d, pl.ds(off, CHUNK), :] += _dot(qeff[d, pl.ds(off, CHUNK), :], sb16)
            st[d] = state * egt[d, c][0:1, :] + _dot(mneg[d, c], sb16) + nmat[d, c]
        return carry

    lax.fori_loop(0, n_chunks, scan, 0)
    sf_ref[0, 0] = st[0]
    sb_ref[0, 0] = st[1]

    z = z_ref[...]
    o_ref[...] = (_rms(obuf[0] + obuf[1], DV_A) * on_ref[...] * (z * jax.nn.sigmoid(z))).astype(BF16)


def _gdn(proj, gates, conv_w, a_log, dt_bias, out_norm, s0f, s0b, batch, seq):
    n_chunks = seq // CHUNK
    unroll = math.gcd(n_chunks, GDN_MAX_UNROLL)
    smem = pl.BlockSpec(memory_space=pltpu.SMEM)
    col = lambda off: pl.BlockSpec((seq, DK_A), lambda b, h, off=off: (b, off + h))
    cw = lambda off: pl.BlockSpec((3, DK_A), lambda b, h, off=off: (0, off + h))
    state = pl.BlockSpec((1, 1, DK_A, DV_A), lambda b, h: (b, h, 0, 0))
    st_shape = jax.ShapeDtypeStruct((batch, H_A, DK_A, DV_A), F32)
    return pl.pallas_call(
        functools.partial(_gdn_kernel, seq=seq, unroll=unroll),
        grid=(batch, H_A),
        in_specs=[smem, smem, col(0), col(H_A), col(2 * H_A), col(3 * H_A), cw(0), cw(H_A), cw(2 * H_A),
                  pl.BlockSpec((1, 4 * H_A, n_chunks, CHUNK), lambda b, h: (b, 0, 0, 0)),
                  state, state, pl.BlockSpec((1, DV_A), lambda b, h: (0, 0))],
        out_specs=[pl.BlockSpec((seq, DV_A), lambda b, h: (b, h)), state, state],
        out_shape=[jax.ShapeDtypeStruct((batch * seq, WV_A), BF16), st_shape, st_shape],
        scratch_shapes=[pltpu.VMEM((seq, DK_A), F32), pltpu.VMEM((seq, DK_A), F32), pltpu.VMEM((seq, DV_A), F32),
                        pltpu.VMEM((2, seq, DV_A), F32), pltpu.VMEM((2, seq, DK_A), BF16),
                        pltpu.VMEM((2, n_chunks, DK_A, DV_A), F32), pltpu.VMEM((2, n_chunks, DK_A, DK_A), BF16),
                        pltpu.VMEM((2, n_chunks, 8, DV_A), F32), pltpu.VMEM((2, DK_A, DV_A), F32)],
        compiler_params=_cparams("parallel", "parallel"),
        name="gdn",
    )(a_log, dt_bias, proj, proj, proj, proj, conv_w, conv_w, conv_w, gates, s0f, s0b, out_norm)


def _pad_cols(w, n):
    return jnp.pad(w, ((0, 0), (0, n - w.shape[1])))


def kernel(x_prompt, x_sample, state_gdn_fwd, state_gdn_bwd, cache_mla_ckv, cache_mla_krope, cache_gqa_k, cache_gqa_v, c, c_ctx, norm_mix, norm_mlp, w_mod, b_mod, w_mlp_in, w_mlp_out, gdn_w_in, gdn_conv, gdn_a_log, gdn_dt_bias, gdn_out_norm, gdn_w_out, mla_w_down, mla_q_lat_norm, mla_kv_lat_norm, mla_w_uq, mla_w_ukv, mla_qn_nope, mla_qn_rope, mla_kn_nope, mla_kn_rope, mla_w_out, gqa_w_in, gqa_q_norm, gqa_k_norm, gqa_w_out):
    bp, sp, d = x_prompt.shape
    bs, ss, _ = x_sample.shape
    past = cache_mla_ckv.shape[2]
    xs = [x_prompt.reshape(bp * sp, d), x_sample.reshape(bs * ss, d)]
    batches, seqs = (bp, bs), (sp, ss)
    rows_per_batch = (None, ss)
    tms = (512, 1024)

    cond = jnp.zeros((MOD_ROWS, d), F32).at[:bs].set(c).at[CTX_ROW].set(c_ctx)
    mods = _adaln(cond, w_mod, b_mod).reshape(DEPTH, MOD_ROWS, N_MOD, d)
    mods = jnp.pad(mods, ((0, 0), (0, 0), (0, MOD_PAD - N_MOD), (0, 0)))

    gqa_tables = _rope_tables(ss, HD_C, HD_C)
    mla_tables = _rope_tables(ss, ROPE_B, LANE)

    new_gdn_f, new_gdn_b, new_ckv, new_kr, new_k, new_v = [], [], [], [], [], []
    for i in range(DEPTH):
        kind, j = i % N_MIXERS, i // N_MIXERS
        mod = mods[i]
        g_mix = norm_mix[i].reshape(1, d)
        g_mlp = norm_mlp[i].reshape(1, d)
        mixed = []
        if kind == 0:
            w_in = gdn_w_in[j]
            n_main = 2 * WK_A + 2 * WV_A
            w_main = w_in[:, :n_main].astype(BF16)
            w_gate = _pad_cols(w_in[:, n_main:], LANE).astype(BF16)
            w_o = gdn_w_out[j].astype(BF16)
            on = gdn_out_norm[j].reshape(1, DV_A)
            for gi in range(2):
                b, s, tm = batches[gi], seqs[gi], tms[gi]
                proj = _normproj(xs[gi], mod, g_mix, w_main, rows_per_batch[gi], tm, 512)
                gb = _normproj(xs[gi], mod, g_mix, w_gate, rows_per_batch[gi], tm, LANE)
                gates = gb[:, :4 * H_A].reshape(b, s // CHUNK, CHUNK, 4 * H_A).transpose(0, 3, 1, 2)
                if gi == 0:
                    s0f = s0b = jnp.zeros((b, H_A, DK_A, DV_A), F32)
                else:
                    s0f, s0b = state_gdn_fwd[:, j], state_gdn_bwd[:, j]
                o, s_f, s_b = _gdn(proj, gates, gdn_conv[j], gdn_a_log[j], gdn_dt_bias[j], on, s0f, s0b, b, s)
                if gi == 0:
                    new_gdn_f.append(s_f)
                    new_gdn_b.append(s_b)
                mixed.append((o, w_o))
        elif kind == 1:
            w_uq = mla_w_uq[j].reshape(Q_LORA, H_B, NOPE_B + ROPE_B)
            w_uq = jnp.pad(w_uq, ((0, 0), (0, 0), (0, SLOT_B - NOPE_B - ROPE_B))).reshape(Q_LORA, H_B * SLOT_B)
            p = dict(q_lat_g=mla_q_lat_norm[j].reshape(1, Q_LORA), kv_lat_g=mla_kv_lat_norm[j].reshape(1, KV_LORA),
                     kn_rope=_pad_cols(mla_kn_rope[j].reshape(1, ROPE_B), LANE),
                     qn_rope=_pad_cols(mla_qn_rope[j].reshape(1, ROPE_B), LANE),
                     qn_nope=mla_qn_nope[j].reshape(1, NOPE_B), kn_nope=mla_kn_nope[j].reshape(1, NOPE_B),
                     w_uq=w_uq.astype(BF16), w_ukv=mla_w_ukv[j].astype(BF16))
            w_down = _pad_cols(mla_w_down[j], DOWN_PAD).astype(BF16)
            w_o = mla_w_out[j].astype(BF16)
            for gi in range(2):
                b, s, tm = batches[gi], seqs[gi], tms[gi]
                proj = _normproj(xs[gi], mod, g_mix, w_down, rows_per_batch[gi], tm, DOWN_PAD)
                q, k, v, ckv, kr = _mla_prep(proj, p, mla_tables if gi == 1 else None, s, 512)
                ctx = None
                if gi == 0:
                    new_ckv.append(ckv.reshape(b, s, KV_LORA))
                    new_kr.append(kr[:, :ROPE_B].reshape(b, s, ROPE_B))
                else:
                    ckv_c = cache_mla_ckv[:, j].reshape(b * past, KV_LORA)
                    kr_c = _pad_cols(cache_mla_krope[:, j].reshape(b * past, ROPE_B), LANE)
                    ctx = _mla_ctx(ckv_c, kr_c, p, 512)
                o = _attention(q, k, v, ctx, batch=b, seq=s, heads_kv=H_B, kv_per_step=4, group=1, dk=SLOT_B,
                               dv=V_B, tq=256)
                mixed.append((o, w_o))
        else:
            w_in = gqa_w_in[j].astype(BF16)
            w_o = gqa_w_out[j].astype(BF16)
            q_g, k_g = gqa_q_norm[j].reshape(1, HD_C), gqa_k_norm[j].reshape(1, HD_C)
            for gi in range(2):
                b, s, tm = batches[gi], seqs[gi], tms[gi]
                proj = _normproj(xs[gi], mod, g_mix, w_in, rows_per_batch[gi], tm, 512)
                q, k, v, kf, vf = _gqa_prep(proj, q_g, k_g, gqa_tables if gi == 1 else None, s, 512)
                ctx = None
                if gi == 0:
                    new_k.append(kf.reshape(b, s, KVH_C, HD_C))
                    new_v.append(vf.reshape(b, s, KVH_C, HD_C))
                else:
                    ctx = (cache_gqa_k[:, j].reshape(b * past, KVH_C * HD_C).astype(BF16),
                           cache_gqa_v[:, j].reshape(b * past, KVH_C * HD_C).astype(BF16))
                o = _attention(q, k, v, ctx, batch=b, seq=s, heads_kv=KVH_C, kv_per_step=2, group=H_C // KVH_C,
                               dk=HD_C, dv=HD_C, tq=256)
                mixed.append((o, w_o))
        w_mi, w_mo = w_mlp_in[i].astype(BF16), w_mlp_out[i].astype(BF16)
        for gi in range(2):
            o, w_o = mixed[gi]
            x_mid = _outproj(xs[gi], o, mod, w_o, rows_per_batch[gi], tms[gi])
            xs[gi] = _mlp(x_mid, mod, g_mlp, w_mi, w_mo, rows_per_batch[gi], tms[gi], 512)

    dt = x_prompt.dtype
    stack = lambda lst: jnp.stack(lst, axis=1).astype(dt)
    return (xs[0].reshape(bp, sp, d), xs[1].reshape(bs, ss, d), stack(new_gdn_f), stack(new_gdn_b),
            stack(new_ckv), stack(new_kr), stack(new_k), stack(new_v))
```

```python
import functools
import math

import jax
import jax.numpy as jnp
from jax import lax
from jax.experimental import pallas as pl
from jax.experimental.pallas import tpu as pltpu

F32 = jnp.float32
BF16 = jnp.bfloat16

D_MODEL = 1024
DEPTH = 4
GRID_W = 64
N_MIXERS = 3
D_FF = 4 * D_MODEL
N_MOD = 6
EPS = 1e-6
ROPE_THETA = 10000.0
H_A = 8
DK_A = 128
DV_A = 128
WK_A = H_A * DK_A
WV_A = H_A * DV_A
CHUNK = 64
H_B = 8
Q_LORA = 384
KV_LORA = 256
NOPE_B = 128
ROPE_B = 64
V_B = 128
SLOT_B = 256
DOWN_PAD = 768
H_C = 8
KVH_C = 2
HD_C = 128

LANE = 128
MOD_ROWS = 16
CTX_ROW = 8
MOD_PAD = 8
VMEM_LIMIT = 56 * 1024 * 1024


def _cparams(*sem):
    return pltpu.CompilerParams(dimension_semantics=sem, vmem_limit_bytes=VMEM_LIMIT)


def _rms(x, n):
    return x * lax.rsqrt(jnp.sum(x * x, axis=-1, keepdims=True) * (1.0 / n) + EPS)


def _dot(a, b):
    return jnp.dot(a, b, preferred_element_type=F32)


def _dot_hi(a, b):
    return jnp.dot(a, b, preferred_element_type=F32, precision=lax.Precision.HIGHEST)


def _dot_nt(a, b):
    return lax.dot_general(a, b, (((1,), (1,)), ((), ())), preferred_element_type=F32)


def _dot_tn(a, b):
    return lax.dot_general(a, b, (((0,), (0,)), ((), ())), preferred_element_type=F32)


def _mod_index(rows_per_batch, tm):
    if rows_per_batch is None:
        return lambda t, *_: (CTX_ROW, 0, 0)
    return lambda t, *_: ((t * tm) // rows_per_batch, 0, 0)


def _adaln_kernel(c_ref, w_ref, b_ref, o_ref):
    c = c_ref[...]
    h = (c * jax.nn.sigmoid(c)).astype(BF16)
    o_ref[0] = _dot(h, w_ref[0].astype(BF16)) + b_ref[0]


def _adaln(cond, w_mod, b_mod):
    tn = 1024
    n = N_MOD * D_MODEL
    return pl.pallas_call(
        _adaln_kernel,
        grid=(DEPTH, n // tn),
        in_specs=[pl.BlockSpec((MOD_ROWS, D_MODEL), lambda i, j: (0, 0)),
                  pl.BlockSpec((1, D_MODEL, tn), lambda i, j: (i, 0, j)),
                  pl.BlockSpec((1, 1, tn), lambda i, j: (i, 0, j))],
        out_specs=pl.BlockSpec((1, MOD_ROWS, tn), lambda i, j: (i, 0, j)),
        out_shape=jax.ShapeDtypeStruct((DEPTH, MOD_ROWS, n), F32),
        compiler_params=_cparams("parallel", "parallel"),
        name="adaln",
    )(cond, w_mod, b_mod.reshape(DEPTH, 1, n))


def _modnorm(x, g, mod, shift_row, scale_row):
    y = _rms(x, D_MODEL) * g
    return y * (1.0 + mod[scale_row:scale_row + 1, :]) + mod[shift_row:shift_row + 1, :]


def _normproj_kernel(*refs, has_aux):
    if has_aux:
        x_ref, mod_ref, g_ref, w_ref, wa_ref, o_ref, oa_ref, h_ref = refs
    else:
        x_ref, mod_ref, g_ref, w_ref, o_ref, h_ref = refs

    @pl.when(pl.program_id(1) == 0)
    def _():
        h_ref[...] = _modnorm(x_ref[...], g_ref[...], mod_ref[0], 0, 1).astype(BF16)
        if has_aux:
            oa_ref[...] = _dot(h_ref[...], wa_ref[...])

    o_ref[...] = _dot(h_ref[...], w_ref[...]).astype(o_ref.dtype)


def _normproj(x, mod, g, w, rows_per_batch, tm, tn, out_dtype=F32, w_aux=None):
    t, n = x.shape[0], w.shape[1]
    in_specs = [pl.BlockSpec((tm, D_MODEL), lambda i, j: (i, 0)),
                pl.BlockSpec((1, MOD_PAD, D_MODEL), _mod_index(rows_per_batch, tm)),
                pl.BlockSpec((1, D_MODEL), lambda i, j: (0, 0)),
                pl.BlockSpec((D_MODEL, tn), lambda i, j: (0, j))]
    out_specs = [pl.BlockSpec((tm, tn), lambda i, j: (i, j))]
    out_shape = [jax.ShapeDtypeStruct((t, n), out_dtype)]
    args = [x, mod, g, w]
    if w_aux is not None:
        in_specs.append(pl.BlockSpec((D_MODEL, LANE), lambda i, j: (0, 0)))
        out_specs.append(pl.BlockSpec((tm, LANE), lambda i, j: (i, 0)))
        out_shape.append(jax.ShapeDtypeStruct((t, LANE), F32))
        args.append(w_aux)
    out = pl.pallas_call(
        functools.partial(_normproj_kernel, has_aux=w_aux is not None),
        grid=(t // tm, n // tn),
        in_specs=in_specs,
        out_specs=out_specs,
        out_shape=out_shape,
        scratch_shapes=[pltpu.VMEM((tm, D_MODEL), BF16)],
        compiler_params=_cparams("parallel", "arbitrary"),
        name="normproj",
    )(*args)
    return out if w_aux is not None else out[0]


def _mix_mlp_kernel(x_ref, a_ref, mod_ref, g_ref, wo_ref, win_ref, wout_ref, o_ref, h_ref, acc_ref):
    f = pl.program_id(1)

    @pl.when(f == 0)
    def _():
        o_ref[...] = x_ref[...] + mod_ref[0, 2:3, :] * _dot(a_ref[...], wo_ref[...])
        h_ref[...] = _modnorm(o_ref[...], g_ref[...], mod_ref[0], 3, 4).astype(BF16)
        acc_ref[...] = jnp.zeros_like(acc_ref)

    a = jnp.maximum(_dot(h_ref[...], win_ref[...]), 0.0)
    acc_ref[...] += _dot((a * a).astype(BF16), wout_ref[...])

    @pl.when(f == pl.num_programs(1) - 1)
    def _():
        o_ref[...] += mod_ref[0, 5:6, :] * acc_ref[...]


def _mix_mlp(x, a, mod, g, w_o, w_in, w_out, rows_per_batch, tm, tf):
    t = x.shape[0]
    return pl.pallas_call(
        _mix_mlp_kernel,
        grid=(t // tm, D_FF // tf),
        in_specs=[pl.BlockSpec((tm, D_MODEL), lambda i, f: (i, 0)),
                  pl.BlockSpec((tm, D_MODEL), lambda i, f: (i, 0)),
                  pl.BlockSpec((1, MOD_PAD, D_MODEL), _mod_index(rows_per_batch, tm)),
                  pl.BlockSpec((1, D_MODEL), lambda i, f: (0, 0)),
                  pl.BlockSpec((D_MODEL, D_MODEL), lambda i, f: (0, 0)),
                  pl.BlockSpec((D_MODEL, tf), lambda i, f: (0, f)),
                  pl.BlockSpec((tf, D_MODEL), lambda i, f: (f, 0))],
        out_specs=pl.BlockSpec((tm, D_MODEL), lambda i, f: (i, 0)),
        out_shape=jax.ShapeDtypeStruct((t, D_MODEL), F32),
        scratch_shapes=[pltpu.VMEM((tm, D_MODEL), BF16), pltpu.VMEM((tm, D_MODEL), F32)],
        compiler_params=_cparams("parallel", "arbitrary"),
        name="mix_mlp",
    )(x, a, mod, g, w_o, w_in, w_out)


def _attn_kernel(*refs, kv_per_step, group, dk, dv, has_ctx):
    if has_ctx:
        q_ref, k_ref, v_ref, kc_ref, vc_ref, o_ref = refs
    else:
        q_ref, k_ref, v_ref, o_ref = refs

    def scores(n):
        j = n // group
        q = q_ref[:, n * dk:(n + 1) * dk]
        s = _dot_nt(q, k_ref[:, j * dk:(j + 1) * dk])
        sc = _dot_nt(q, kc_ref[:, j * dk:(j + 1) * dk]) if has_ctx else None
        return s, sc

    heads = kv_per_step * group
    nxt = scores(0)
    for n in range(heads):
        s, sc = nxt
        if n + 1 < heads:
            nxt = scores(n + 1)
        j = n // group
        m = jnp.max(s, axis=-1, keepdims=True)
        if has_ctx:
            m = jnp.maximum(m, jnp.max(sc, axis=-1, keepdims=True))
            pc = jnp.exp(sc - m)
        p = jnp.exp(s - m)
        l = jnp.sum(p, axis=-1, keepdims=True)
        o = _dot(p.astype(BF16), v_ref[:, j * dv:(j + 1) * dv])
        if has_ctx:
            l = l + jnp.sum(pc, axis=-1, keepdims=True)
            o = o + _dot(pc.astype(BF16), vc_ref[:, j * dv:(j + 1) * dv])
        o_ref[:, n * dv:(n + 1) * dv] = (o / l).astype(o_ref.dtype)


def _attention(q, k, v, ctx, *, batch, seq, heads_kv, kv_per_step, group, dk, dv, tq):
    nq = seq // tq
    hp = kv_per_step
    in_specs = [pl.BlockSpec((tq, hp * group * dk), lambda b, j, i: (b * nq + i, j)),
                pl.BlockSpec((seq, hp * dk), lambda b, j, i: (b, j)),
                pl.BlockSpec((seq, hp * dv), lambda b, j, i: (b, j))]
    args = [q, k, v]
    if ctx is not None:
        past = ctx[0].shape[0] // batch
        in_specs += [pl.BlockSpec((past, hp * dk), lambda b, j, i: (b, j)),
                     pl.BlockSpec((past, hp * dv), lambda b, j, i: (b, j))]
        args += list(ctx)
    return pl.pallas_call(
        functools.partial(_attn_kernel, kv_per_step=hp, group=group, dk=dk, dv=dv, has_ctx=ctx is not None),
        grid=(batch, heads_kv // hp, nq),
        in_specs=in_specs,
        out_specs=pl.BlockSpec((tq, hp * group * dv), lambda b, j, i: (b * nq + i, j)),
        out_shape=jax.ShapeDtypeStruct((batch * seq, heads_kv * group * dv), BF16),
        compiler_params=_cparams("parallel", "parallel", "parallel"),
        name="attention",
    )(*args)


def _swap_quarters(x, quarter):
    lanes = x.shape[-1]
    lane = lax.broadcasted_iota(jnp.int32, x.shape, x.ndim - 1)
    up = pltpu.roll(x, lanes - quarter, x.ndim - 1)
    down = pltpu.roll(x, quarter, x.ndim - 1)
    return jnp.where((lane // quarter) % 2 == 0, up, down)


def _rope(x, cos, sin, quarter):
    return x * cos + _swap_quarters(x, quarter) * sin


def _rope_tables(seq, width, pad_to):
    rows = seq // GRID_W
    row = jnp.repeat(jnp.arange(rows, dtype=F32), GRID_W)
    col = jnp.tile(jnp.arange(GRID_W, dtype=F32), rows)
    quarter = width // 4
    freqs = ROPE_THETA ** (-jnp.arange(quarter, dtype=F32) / quarter)
    ar = row[:, None] * freqs[None, :]
    ac = col[:, None] * freqs[None, :]
    cos = jnp.concatenate([jnp.cos(ar), jnp.cos(ar), jnp.cos(ac), jnp.cos(ac)], axis=-1)
    sin = jnp.concatenate([-jnp.sin(ar), jnp.sin(ar), -jnp.sin(ac), jnp.sin(ac)], axis=-1)
    pad = ((0, 0), (0, pad_to - width))
    return jnp.pad(cos, pad), jnp.pad(sin, pad)


def _gqa_prep_kernel(*refs, use_rope):
    if use_rope:
        p_ref, qg_ref, kg_ref, cos_ref, sin_ref, q_ref, k_ref, v_ref, kf_ref, vf_ref = refs
    else:
        p_ref, qg_ref, kg_ref, q_ref, k_ref, v_ref, kf_ref, vf_ref = refs
    for h in range(H_C + KVH_C):
        x = _rms(p_ref[:, h * HD_C:(h + 1) * HD_C], HD_C)
        x = x * (qg_ref[...] if h < H_C else kg_ref[...])
        if h >= H_C:
            kf_ref[:, (h - H_C) * HD_C:(h - H_C + 1) * HD_C] = x
        if use_rope:
            x = _rope(x, cos_ref[...], sin_ref[...], HD_C // 4)
        if h < H_C:
            q_ref[:, h * HD_C:(h + 1) * HD_C] = (x * HD_C ** -0.5).astype(BF16)
        else:
            k_ref[:, (h - H_C) * HD_C:(h - H_C + 1) * HD_C] = x.astype(BF16)
    vv = p_ref[:, (H_C + KVH_C) * HD_C:]
    vf_ref[...] = vv
    v_ref[...] = vv.astype(BF16)


def _gqa_prep(proj, q_g, k_g, tables, seq, tm):
    t = proj.shape[0]
    nq, nkv = H_C * HD_C, KVH_C * HD_C
    in_specs = [pl.BlockSpec((tm, nq + 2 * nkv), lambda i: (i, 0)),
                pl.BlockSpec((1, HD_C), lambda i: (0, 0)),
                pl.BlockSpec((1, HD_C), lambda i: (0, 0))]
    args = [proj, q_g, k_g]
    if tables is not None:
        per = seq // tm
        in_specs += [pl.BlockSpec((tm, HD_C), lambda i: (i % per, 0))] * 2
        args += list(tables)
    return pl.pallas_call(
        functools.partial(_gqa_prep_kernel, use_rope=tables is not None),
        grid=(t // tm,),
        in_specs=in_specs,
        out_specs=[pl.BlockSpec((tm, nq), lambda i: (i, 0))] + [pl.BlockSpec((tm, nkv), lambda i: (i, 0))] * 4,
        out_shape=[jax.ShapeDtypeStruct((t, nq), BF16), jax.ShapeDtypeStruct((t, nkv), BF16),
                   jax.ShapeDtypeStruct((t, nkv), BF16), jax.ShapeDtypeStruct((t, nkv), F32),
                   jax.ShapeDtypeStruct((t, nkv), F32)],
        compiler_params=_cparams("parallel"),
        name="gqa_prep",
    )(*args)


def _mla_keys_values(ckv, krope_slot, wukv_ref, knn_ref, k_ref, v_ref):
    kv = _dot(ckv.astype(BF16), wukv_ref[...])
    kr = krope_slot.astype(BF16)
    for h in range(H_B):
        base = h * (NOPE_B + V_B)
        kn = _rms(kv[:, base:base + NOPE_B], NOPE_B) * knn_ref[...]
        k_ref[:, h * SLOT_B:h * SLOT_B + NOPE_B] = kn.astype(BF16)
        k_ref[:, h * SLOT_B + NOPE_B:(h + 1) * SLOT_B] = kr
        v_ref[:, h * V_B:(h + 1) * V_B] = kv[:, base + NOPE_B:base + NOPE_B + V_B].astype(BF16)


def _mla_prep_kernel(*refs, use_rope):
    if use_rope:
        (p_ref, qlg_ref, kvg_ref, krg_ref, wuq_ref, wukv_ref, qnn_ref, qnr_ref, knn_ref, cos_ref, sin_ref,
         q_ref, k_ref, v_ref, ckv_ref, kr_ref) = refs
    else:
        (p_ref, qlg_ref, kvg_ref, krg_ref, wuq_ref, wukv_ref, qnn_ref, qnr_ref, knn_ref,
         q_ref, k_ref, v_ref, ckv_ref, kr_ref) = refs
    cq = _rms(p_ref[:, :Q_LORA], Q_LORA) * qlg_ref[...]
    ckv = _rms(p_ref[:, Q_LORA:Q_LORA + KV_LORA], KV_LORA) * kvg_ref[...]
    kr = _rms(p_ref[:, Q_LORA + KV_LORA:], ROPE_B) * krg_ref[...]
    ckv_ref[...] = ckv
    kr_ref[...] = kr
    q = _dot(cq.astype(BF16), wuq_ref[...])
    scale = (NOPE_B + ROPE_B) ** -0.5
    for h in range(H_B):
        qn = _rms(q[:, h * SLOT_B:h * SLOT_B + NOPE_B], NOPE_B) * qnn_ref[...]
        qr = _rms(q[:, h * SLOT_B + NOPE_B:(h + 1) * SLOT_B], ROPE_B) * qnr_ref[...]
        if use_rope:
            qr = _rope(qr, cos_ref[...], sin_ref[...], ROPE_B // 4)
        q_ref[:, h * SLOT_B:h * SLOT_B + NOPE_B] = (qn * scale).astype(BF16)
        q_ref[:, h * SLOT_B + NOPE_B:(h + 1) * SLOT_B] = (qr * scale).astype(BF16)
    if use_rope:
        kr = _rope(kr, cos_ref[...], sin_ref[...], ROPE_B // 4)
    _mla_keys_values(ckv, kr, wukv_ref, knn_ref, k_ref, v_ref)


def _mla_prep(proj, p, tables, seq, tm):
    t = proj.shape[0]
    full = lambda shape: pl.BlockSpec(shape, lambda i: (0, 0))
    in_specs = [pl.BlockSpec((tm, DOWN_PAD), lambda i: (i, 0)),
                full((1, Q_LORA)), full((1, KV_LORA)), full((1, LANE)),
                full((Q_LORA, H_B * SLOT_B)), full((KV_LORA, H_B * (NOPE_B + V_B))),
                full((1, NOPE_B)), full((1, LANE)), full((1, NOPE_B))]
    args = [proj, p["q_lat_g"], p["kv_lat_g"], p["kn_rope"], p["w_uq"], p["w_ukv"], p["qn_nope"], p["qn_rope"],
            p["kn_nope"]]
    if tables is not None:
        per = seq // tm
        in_specs += [pl.BlockSpec((tm, LANE), lambda i: (i % per, 0))] * 2
        args += list(tables)
    row = lambda n: pl.BlockSpec((tm, n), lambda i: (i, 0))
    return pl.pallas_call(
        functools.partial(_mla_prep_kernel, use_rope=tables is not None),
        grid=(t // tm,),
        in_specs=in_specs,
        out_specs=[row(H_B * SLOT_B), row(H_B * SLOT_B), row(H_B * V_B), row(KV_LORA), row(LANE)],
        out_shape=[jax.ShapeDtypeStruct((t, H_B * SLOT_B), BF16), jax.ShapeDtypeStruct((t, H_B * SLOT_B), BF16),
                   jax.ShapeDtypeStruct((t, H_B * V_B), BF16), jax.ShapeDtypeStruct((t, KV_LORA), F32),
                   jax.ShapeDtypeStruct((t, LANE), F32)],
        compiler_params=_cparams("parallel"),
        name="mla_prep",
    )(*args)


def _mla_ctx_kernel(ckv_ref, kr_ref, wukv_ref, knn_ref, k_ref, v_ref):
    _mla_keys_values(ckv_ref[...], kr_ref[...], wukv_ref, knn_ref, k_ref, v_ref)


def _mla_ctx(ckv, krope_slot, p, tm):
    t = ckv.shape[0]
    full = lambda shape: pl.BlockSpec(shape, lambda i: (0, 0))
    row = lambda n: pl.BlockSpec((tm, n), lambda i: (i, 0))
    return pl.pallas_call(
        _mla_ctx_kernel,
        grid=(t // tm,),
        in_specs=[row(KV_LORA), row(LANE), full((KV_LORA, H_B * (NOPE_B + V_B))), full((1, NOPE_B))],
        out_specs=[row(H_B * SLOT_B), row(H_B * V_B)],
        out_shape=[jax.ShapeDtypeStruct((t, H_B * SLOT_B), BF16), jax.ShapeDtypeStruct((t, H_B * V_B), BF16)],
        compiler_params=_cparams("parallel"),
        name="mla_ctx",
    )(ckv, krope_slot, p["w_ukv"], p["kn_nope"])


def _conv_silu(x, w):
    s = x.shape[0]
    row = lax.broadcasted_iota(jnp.int32, x.shape, 0)
    prev = jnp.where(row == 0, 0.0, pltpu.roll(x, 1, 0))
    nxt = jnp.where(row == s - 1, 0.0, pltpu.roll(x, s - 1, 0))
    y = prev * w[0:1, :] + x * w[1:2, :] + nxt * w[2:3, :]
    return y * jax.nn.sigmoid(y)


def _l2n(x):
    return x * lax.rsqrt(jnp.sum(x * x, axis=-1, keepdims=True) + EPS)


def _softplus(x):
    return jnp.maximum(x, 0.0) + jnp.log1p(jnp.exp(-jnp.abs(x)))


GDN_HEADS = 2
GDN_CHAINS = 16


def _gdn_kernel(alog_ref, dtb_ref, q_ref, k_ref, v_ref, z_ref, cq_ref, ck_ref, cv_ref, gate_ref,
                s0f_ref, s0b_ref, on_ref, o_ref, sf_ref, sb_ref,
                qs, ks, vs, obuf, qeff, nmat, mneg, egt, st, *, seq, unroll):
    h0 = pl.program_id(1) * GDN_HEADS
    n_chunks = seq // CHUNK
    for hh in range(GDN_HEADS):
        cols = slice(hh * DK_A, (hh + 1) * DK_A)
        qs[hh] = _l2n(_conv_silu(q_ref[:, cols].astype(F32), cq_ref[:, cols])) * (DK_A ** -0.5)
        ks[hh] = _l2n(_conv_silu(k_ref[:, cols].astype(F32), ck_ref[:, cols]))
        vs[hh] = _conv_silu(v_ref[:, cols].astype(F32), cv_ref[:, cols])

    ii = lax.broadcasted_iota(jnp.int32, (CHUNK, CHUNK), 0)
    jj = lax.broadcasted_iota(jnp.int32, (CHUNK, CHUNK), 1)
    eye = ii == jj
    incl = (ii >= jj, ii <= jj)
    strict = (ii > jj, ii < jj)
    last = (CHUNK - 1, 0)
    pair = [(ii // 2) == (jj // 2)]
    pair += [((ii // (2 * s)) == (jj // (2 * s))) & ((ii // s) != (jj // s)) for s in (2, 4, 8, 16, 32)]
    slots = [(hh, d) for hh in range(GDN_HEADS) for d in range(2)]
    neg_a = [-jnp.exp(jnp.full((1, CHUNK), alog_ref[d, h0 + hh], F32)) for hh, d in slots]
    dtb = [dtb_ref[d, h0 + hh] for hh, d in slots]

    def to_col(r):
        return jnp.sum(jnp.where(eye, jnp.broadcast_to(r, (CHUNK, CHUNK)), 0.0), axis=1, keepdims=True)

    def to_row(c):
        return jnp.sum(jnp.where(eye, jnp.broadcast_to(c, (CHUNK, CHUNK)), 0.0), axis=0, keepdims=True)

    def prepare(blk, carry):
        cs = [blk * unroll + u for u in range(unroll)]
        offs = [pl.multiple_of(c * CHUNK, CHUNK) for c in cs]
        tiles = [(hh, u) for hh in range(GDN_HEADS) for u in range(unroll)]
        qc = {t: qs[t[0], pl.ds(offs[t[1]], CHUNK), :] for t in tiles}
        kc = {t: ks[t[0], pl.ds(offs[t[1]], CHUNK), :] for t in tiles}
        vc = {t: vs[t[0], pl.ds(offs[t[1]], CHUNK), :] for t in tiles}
        kcb = {t: kc[t].astype(BF16) for t in tiles}
        kk = {t: _dot_nt(kcb[t], kcb[t]) for t in tiles}
        qk_raw = {t: _dot_nt(qc[t].astype(BF16), kcb[t]) for t in tiles}
        chains = [(hh, u, d) for hh, u in tiles for d in range(2)]
        beta_col, gcum_col, g_tot, eg, qk, a_mat = [], [], [], [], [], []
        for hh, u, d in chains:
            c, slot = cs[u], 2 * hh + d
            g_raw = gate_ref[0, d * H_A + h0 + hh, pl.ds(c, 1), :]
            b_raw = gate_ref[0, 2 * H_A + d * H_A + h0 + hh, pl.ds(c, 1), :]
            g_row = neg_a[slot] * _softplus(g_raw + dtb[slot])
            beta_col.append(to_col(jax.nn.sigmoid(b_raw)))
            gcum_col.append(jnp.sum(jnp.where(incl[d], jnp.broadcast_to(g_row, (CHUNK, CHUNK)), 0.0), axis=1,
                                    keepdims=True))
            g_tot.append(gcum_col[-1][last[d]:last[d] + 1, :])
            decay = jnp.exp(jnp.where(incl[d], gcum_col[-1] - to_row(gcum_col[-1]), -jnp.inf))
            eg.append(jnp.exp(gcum_col[-1]))
            qk.append((qk_raw[hh, u] * decay).astype(BF16))
            a_mat.append(jnp.where(strict[d], kk[hh, u] * beta_col[-1] * decay, 0.0))
        tm = [jnp.where(eye, 1.0, -jnp.where(pair[0], a, 0.0)) for a in a_mat]
        for lvl in range(1, len(pair)):
            tb = [t.astype(BF16) for t in tm]
            x = [_dot(jnp.where(pair[lvl], a, 0.0).astype(BF16), t) for a, t in zip(a_mat, tb)]
            tm = [t - _dot(t16, xi.astype(BF16)) for t, t16, xi in zip(tm, tb, x)]
        nm = [jnp.where(eye, 0.0, t) for t in tm]
        rhs = [jnp.concatenate([vc[hh, u] * beta_col[i], kc[hh, u] * (beta_col[i] * eg[i])], axis=-1)
               for i, (hh, u, d) in enumerate(chains)]
        sol = [(r + _dot(n.astype(BF16), r.astype(BF16))).astype(BF16) for n, r in zip(nm, rhs)]
        local = [_dot(a, b) for a, b in zip(qk, sol)]
        k_dec = [(kc[hh, u] * jnp.exp(g_tot[i] - gcum_col[i])).astype(BF16)
                 for i, (hh, u, d) in enumerate(chains)]
        nm_mat = [_dot_tn(a, b) for a, b in zip(k_dec, sol)]
        for i, (hh, u, d) in enumerate(chains):
            slot = 2 * hh + d
            obuf[slot, pl.ds(offs[u], CHUNK), :] = local[i][:, :DV_A]
            qeff[slot, pl.ds(offs[u], CHUNK), :] = (qc[hh, u] * eg[i] - local[i][:, DV_A:]).astype(BF16)
            nmat[slot, cs[u]] = nm_mat[i][:, :DV_A]
            mneg[slot, cs[u]] = (-nm_mat[i][:, DV_A:]).astype(BF16)
            egt[slot, cs[u]] = jnp.broadcast_to(jnp.exp(g_tot[i]), (8, DV_A))
        return carry

    lax.fori_loop(0, n_chunks // unroll, prepare, 0)

    for hh, d in slots:
        st[2 * hh + d] = (s0f_ref, s0b_ref)[d][0, hh]

    def scan(ci, carry):
        states = [st[slot] for slot in range(len(slots))]
        sb16 = [s.astype(BF16) for s in states]
        cs = [ci if d == 0 else n_chunks - 1 - ci for hh, d in slots]
        for slot, c in enumerate(cs):
            st[slot] = states[slot] * egt[slot, c][0:1, :] + _dot(mneg[slot, c], sb16[slot]) + nmat[slot, c]
        for slot, c in enumerate(cs):
            off = pl.multiple_of(c * CHUNK, CHUNK)
            obuf[slot, pl.ds(off, CHUNK), :] += _dot(qeff[slot, pl.ds(off, CHUNK), :], sb16[slot])
        return carry

    lax.fori_loop(0, n_chunks, scan, 0)
    for hh, d in slots:
        (sf_ref, sb_ref)[d][0, hh] = st[2 * hh + d]

    for hh in range(GDN_HEADS):
        cols = slice(hh * DV_A, (hh + 1) * DV_A)
        z = z_ref[:, cols].astype(F32)
        o = _rms(obuf[2 * hh] + obuf[2 * hh + 1], DV_A) * on_ref[...] * (z * jax.nn.sigmoid(z))
        o_ref[:, cols] = o.astype(BF16)


def _gdn(proj, gates, conv_w, a_log, dt_bias, out_norm, s0f, s0b, batch, seq):
    n_chunks = seq // CHUNK
    unroll = math.gcd(n_chunks, GDN_CHAINS // (2 * GDN_HEADS))
    hb, groups, slots = GDN_HEADS, H_A // GDN_HEADS, 2 * GDN_HEADS
    smem = pl.BlockSpec(memory_space=pltpu.SMEM)
    col = lambda off: pl.BlockSpec((seq, hb * DK_A), lambda b, h, off=off: (b, off + h))
    cw = lambda off: pl.BlockSpec((3, hb * DK_A), lambda b, h, off=off: (0, off + h))
    state = pl.BlockSpec((1, hb, DK_A, DV_A), lambda b, h: (b, h, 0, 0))
    st_shape = jax.ShapeDtypeStruct((batch, H_A, DK_A, DV_A), F32)
    return pl.pallas_call(
        functools.partial(_gdn_kernel, seq=seq, unroll=unroll),
        grid=(batch, groups),
        in_specs=[smem, smem, col(0), col(groups), col(2 * groups), col(3 * groups),
                  cw(0), cw(groups), cw(2 * groups),
                  pl.BlockSpec((1, 4 * H_A, n_chunks, CHUNK), lambda b, h: (b, 0, 0, 0)),
                  state, state, pl.BlockSpec((1, DV_A), lambda b, h: (0, 0))],
        out_specs=[pl.BlockSpec((seq, hb * DV_A), lambda b, h: (b, h)), state, state],
        out_shape=[jax.ShapeDtypeStruct((batch * seq, WV_A), BF16), st_shape, st_shape],
        scratch_shapes=[pltpu.VMEM((hb, seq, DK_A), F32), pltpu.VMEM((hb, seq, DK_A), F32),
                        pltpu.VMEM((hb, seq, DV_A), F32),
                        pltpu.VMEM((slots, seq, DV_A), F32), pltpu.VMEM((slots, seq, DK_A), BF16),
                        pltpu.VMEM((slots, n_chunks, DK_A, DV_A), F32),
                        pltpu.VMEM((slots, n_chunks, DK_A, DK_A), BF16),
                        pltpu.VMEM((slots, n_chunks, 8, DV_A), F32), pltpu.VMEM((slots, DK_A, DV_A), F32)],
        compiler_params=_cparams("parallel", "parallel"),
        name="gdn",
    )(a_log, dt_bias, proj, proj, proj, proj, conv_w, conv_w, conv_w, gates, s0f, s0b, out_norm)


def _pad_cols(w, n):
    return jnp.pad(w, ((0, 0), (0, n - w.shape[1])))


def kernel(x_prompt, x_sample, state_gdn_fwd, state_gdn_bwd, cache_mla_ckv, cache_mla_krope, cache_gqa_k, cache_gqa_v, c, c_ctx, norm_mix, norm_mlp, w_mod, b_mod, w_mlp_in, w_mlp_out, gdn_w_in, gdn_conv, gdn_a_log, gdn_dt_bias, gdn_out_norm, gdn_w_out, mla_w_down, mla_q_lat_norm, mla_kv_lat_norm, mla_w_uq, mla_w_ukv, mla_qn_nope, mla_qn_rope, mla_kn_nope, mla_kn_rope, mla_w_out, gqa_w_in, gqa_q_norm, gqa_k_norm, gqa_w_out):
    bp, sp, d = x_prompt.shape
    bs, ss, _ = x_sample.shape
    past = cache_mla_ckv.shape[2]
    xs = [x_prompt.reshape(bp * sp, d), x_sample.reshape(bs * ss, d)]
    batches, seqs = (bp, bs), (sp, ss)
    rows_per_batch = (None, ss)
    tms = (512, 1024)

    cond = jnp.zeros((MOD_ROWS, d), F32).at[:bs].set(c).at[CTX_ROW].set(c_ctx)
    mods = _adaln(cond, w_mod, b_mod).reshape(DEPTH, MOD_ROWS, N_MOD, d)
    mods = jnp.pad(mods, ((0, 0), (0, 0), (0, MOD_PAD - N_MOD), (0, 0)))

    gqa_tables = _rope_tables(ss, HD_C, HD_C)
    mla_tables = _rope_tables(ss, ROPE_B, LANE)

    new_gdn_f, new_gdn_b, new_ckv, new_kr, new_k, new_v = [], [], [], [], [], []
    for i in range(DEPTH):
        kind, j = i % N_MIXERS, i // N_MIXERS
        mod = mods[i]
        g_mix = norm_mix[i].reshape(1, d)
        g_mlp = norm_mlp[i].reshape(1, d)
        mixed = []
        if kind == 0:
            w_in = gdn_w_in[j]
            n_main = 2 * WK_A + 2 * WV_A
            w_main = w_in[:, :n_main].astype(BF16)
            w_gate = _pad_cols(w_in[:, n_main:], LANE).astype(BF16)
            w_o = gdn_w_out[j].astype(BF16)
            on = gdn_out_norm[j].reshape(1, DV_A)
            for gi in range(2):
                b, s, tm = batches[gi], seqs[gi], tms[gi]
                proj, gb = _normproj(xs[gi], mod, g_mix, w_main, rows_per_batch[gi], tm, 1024, BF16, w_gate)
                gates = gb[:, :4 * H_A].reshape(b, s // CHUNK, CHUNK, 4 * H_A).transpose(0, 3, 1, 2)
                if gi == 0:
                    s0f = s0b = jnp.zeros((b, H_A, DK_A, DV_A), F32)
                else:
                    s0f, s0b = state_gdn_fwd[:, j], state_gdn_bwd[:, j]
                o, s_f, s_b = _gdn(proj, gates, gdn_conv[j], gdn_a_log[j], gdn_dt_bias[j], on, s0f, s0b, b, s)
                if gi == 0:
                    new_gdn_f.append(s_f)
                    new_gdn_b.append(s_b)
                mixed.append((o, w_o))
        elif kind == 1:
            w_uq = mla_w_uq[j].reshape(Q_LORA, H_B, NOPE_B + ROPE_B)
            w_uq = jnp.pad(w_uq, ((0, 0), (0, 0), (0, SLOT_B - NOPE_B - ROPE_B))).reshape(Q_LORA, H_B * SLOT_B)
            p = dict(q_lat_g=mla_q_lat_norm[j].reshape(1, Q_LORA), kv_lat_g=mla_kv_lat_norm[j].reshape(1, KV_LORA),
                     kn_rope=_pad_cols(mla_kn_rope[j].reshape(1, ROPE_B), LANE),
                     qn_rope=_pad_cols(mla_qn_rope[j].reshape(1, ROPE_B), LANE),
                     qn_nope=mla_qn_nope[j].reshape(1, NOPE_B), kn_nope=mla_kn_nope[j].reshape(1, NOPE_B),
                     w_uq=w_uq.astype(BF16), w_ukv=mla_w_ukv[j].astype(BF16))
            w_down = _pad_cols(mla_w_down[j], DOWN_PAD).astype(BF16)
            w_o = mla_w_out[j].astype(BF16)
            for gi in range(2):
                b, s, tm = batches[gi], seqs[gi], tms[gi]
                proj = _normproj(xs[gi], mod, g_mix, w_down, rows_per_batch[gi], tm, DOWN_PAD)
                q, k, v, ckv, kr = _mla_prep(proj, p, mla_tables if gi == 1 else None, s, 512)
                ctx = None
                if gi == 0:
                    new_ckv.append(ckv.reshape(b, s, KV_LORA))
                    new_kr.append(kr[:, :ROPE_B].reshape(b, s, ROPE_B))
                else:
                    ckv_c = cache_mla_ckv[:, j].reshape(b * past, KV_LORA)
                    kr_c = _pad_cols(cache_mla_krope[:, j].reshape(b * past, ROPE_B), LANE)
                    ctx = _mla_ctx(ckv_c, kr_c, p, 512)
                o = _attention(q, k, v, ctx, batch=b, seq=s, heads_kv=H_B, kv_per_step=4, group=1, dk=SLOT_B,
                               dv=V_B, tq=256)
                mixed.append((o, w_o))
        else:
            w_in = gqa_w_in[j].astype(BF16)
            w_o = gqa_w_out[j].astype(BF16)
            q_g, k_g = gqa_q_norm[j].reshape(1, HD_C), gqa_k_norm[j].reshape(1, HD_C)
            for gi in range(2):
                b, s, tm = batches[gi], seqs[gi], tms[gi]
                proj = _normproj(xs[gi], mod, g_mix, w_in, rows_per_batch[gi], tm, 512)
                q, k, v, kf, vf = _gqa_prep(proj, q_g, k_g, gqa_tables if gi == 1 else None, s, 512)
                ctx = None
                if gi == 0:
                    new_k.append(kf.reshape(b, s, KVH_C, HD_C))
                    new_v.append(vf.reshape(b, s, KVH_C, HD_C))
                else:
                    ctx = (cache_gqa_k[:, j].reshape(b * past, KVH_C * HD_C).astype(BF16),
                           cache_gqa_v[:, j].reshape(b * past, KVH_C * HD_C).astype(BF16))
                o = _attention(q, k, v, ctx, batch=b, seq=s, heads_kv=KVH_C, kv_per_step=2, group=H_C // KVH_C,
                               dk=HD_C, dv=HD_C, tq=256)
                mixed.append((o, w_o))
        w_mi, w_mo = w_mlp_in[i].astype(BF16), w_mlp_out[i].astype(BF16)
        for gi in range(2):
            o, w_o = mixed[gi]
            xs[gi] = _mix_mlp(xs[gi], o, mod, g_mlp, w_o, w_mi, w_mo, rows_per_batch[gi], tms[gi], 1024)

    dt = x_prompt.dtype
    stack = lambda lst: jnp.stack(lst, axis=1).astype(dt)
    return (xs[0].reshape(bp, sp, d), xs[1].reshape(bs, ss, d), stack(new_gdn_f), stack(new_gdn_b),
            stack(new_ckv), stack(new_kr), stack(new_k), stack(new_v))
```

```python
import functools
import math

import jax
import jax.numpy as jnp
from jax import lax
from jax.experimental import pallas as pl
from jax.experimental.pallas import tpu as pltpu

F32 = jnp.float32
BF16 = jnp.bfloat16

D_MODEL = 1024
DEPTH = 4
GRID_W = 64
N_MIXERS = 3
D_FF = 4 * D_MODEL
N_MOD = 6
EPS = 1e-6
ROPE_THETA = 10000.0
H_A = 8
DK_A = 128
DV_A = 128
WK_A = H_A * DK_A
WV_A = H_A * DV_A
CHUNK = 64
H_B = 8
Q_LORA = 384
KV_LORA = 256
NOPE_B = 128
ROPE_B = 64
V_B = 128
SLOT_B = 256
DOWN_PAD = 768
H_C = 8
KVH_C = 2
HD_C = 128

LANE = 128
MOD_ROWS = 16
CTX_ROW = 8
MOD_PAD = 8
VMEM_LIMIT = 56 * 1024 * 1024


def _cparams(*sem):
    return pltpu.CompilerParams(dimension_semantics=sem, vmem_limit_bytes=VMEM_LIMIT)


def _rms(x, n):
    return x * lax.rsqrt(jnp.sum(x * x, axis=-1, keepdims=True) * (1.0 / n) + EPS)


def _dot(a, b):
    return jnp.dot(a, b, preferred_element_type=F32)


def _dot_hi(a, b):
    return jnp.dot(a, b, preferred_element_type=F32, precision=lax.Precision.HIGHEST)


def _dot_nt(a, b):
    return lax.dot_general(a, b, (((1,), (1,)), ((), ())), preferred_element_type=F32)


def _dot_tn(a, b):
    return lax.dot_general(a, b, (((0,), (0,)), ((), ())), preferred_element_type=F32)


def _mod_index(rows_per_batch, tm):
    if rows_per_batch is None:
        return lambda t, *_: (CTX_ROW, 0, 0)
    return lambda t, *_: ((t * tm) // rows_per_batch, 0, 0)


def _adaln_kernel(c_ref, w_ref, b_ref, o_ref):
    c = c_ref[...]
    h = (c * jax.nn.sigmoid(c)).astype(BF16)
    o_ref[0] = _dot(h, w_ref[0].astype(BF16)) + b_ref[0]


def _adaln(cond, w_mod, b_mod):
    tn = 1024
    n = N_MOD * D_MODEL
    return pl.pallas_call(
        _adaln_kernel,
        grid=(DEPTH, n // tn),
        in_specs=[pl.BlockSpec((MOD_ROWS, D_MODEL), lambda i, j: (0, 0)),
                  pl.BlockSpec((1, D_MODEL, tn), lambda i, j: (i, 0, j)),
                  pl.BlockSpec((1, 1, tn), lambda i, j: (i, 0, j))],
        out_specs=pl.BlockSpec((1, MOD_ROWS, tn), lambda i, j: (i, 0, j)),
        out_shape=jax.ShapeDtypeStruct((DEPTH, MOD_ROWS, n), F32),
        compiler_params=_cparams("parallel", "parallel"),
        name="adaln",
    )(cond, w_mod, b_mod.reshape(DEPTH, 1, n))


def _modnorm(x, g, mod, shift_row, scale_row):
    y = _rms(x, D_MODEL) * g
    return y * (1.0 + mod[scale_row:scale_row + 1, :]) + mod[shift_row:shift_row + 1, :]


def _normproj_kernel(*refs, has_aux):
    if has_aux:
        x_ref, mod_ref, g_ref, w_ref, wa_ref, o_ref, oa_ref, h_ref = refs
    else:
        x_ref, mod_ref, g_ref, w_ref, o_ref, h_ref = refs

    @pl.when(pl.program_id(1) == 0)
    def _():
        h_ref[...] = _modnorm(x_ref[...], g_ref[...], mod_ref[0], 0, 1).astype(BF16)
        if has_aux:
            oa_ref[...] = _dot(h_ref[...], wa_ref[...])

    o_ref[...] = _dot(h_ref[...], w_ref[...]).astype(o_ref.dtype)


def _normproj(x, mod, g, w, rows_per_batch, tm, tn, out_dtype=F32, w_aux=None):
    t, n = x.shape[0], w.shape[1]
    in_specs = [pl.BlockSpec((tm, D_MODEL), lambda i, j: (i, 0)),
                pl.BlockSpec((1, MOD_PAD, D_MODEL), _mod_index(rows_per_batch, tm)),
                pl.BlockSpec((1, D_MODEL), lambda i, j: (0, 0)),
                pl.BlockSpec((D_MODEL, tn), lambda i, j: (0, j))]
    out_specs = [pl.BlockSpec((tm, tn), lambda i, j: (i, j))]
    out_shape = [jax.ShapeDtypeStruct((t, n), out_dtype)]
    args = [x, mod, g, w]
    if w_aux is not None:
        in_specs.append(pl.BlockSpec((D_MODEL, LANE), lambda i, j: (0, 0)))
        out_specs.append(pl.BlockSpec((tm, LANE), lambda i, j: (i, 0)))
        out_shape.append(jax.ShapeDtypeStruct((t, LANE), F32))
        args.append(w_aux)
    out = pl.pallas_call(
        functools.partial(_normproj_kernel, has_aux=w_aux is not None),
        grid=(t // tm, n // tn),
        in_specs=in_specs,
        out_specs=out_specs,
        out_shape=out_shape,
        scratch_shapes=[pltpu.VMEM((tm, D_MODEL), BF16)],
        compiler_params=_cparams("parallel", "arbitrary"),
        name="normproj",
    )(*args)
    return out if w_aux is not None else out[0]


def _mix_mlp_kernel(x_ref, a_ref, mod_ref, g_ref, wo_ref, win_ref, wout_ref, o_ref, h_ref, acc_ref):
    f = pl.program_id(1)

    @pl.when(f == 0)
    def _():
        o_ref[...] = x_ref[...] + mod_ref[0, 2:3, :] * _dot(a_ref[...], wo_ref[...])
        h_ref[...] = _modnorm(o_ref[...], g_ref[...], mod_ref[0], 3, 4).astype(BF16)
        acc_ref[...] = jnp.zeros_like(acc_ref)

    a = jnp.maximum(_dot(h_ref[...], win_ref[...]), 0.0)
    acc_ref[...] += _dot((a * a).astype(BF16), wout_ref[...])

    @pl.when(f == pl.num_programs(1) - 1)
    def _():
        o_ref[...] += mod_ref[0, 5:6, :] * acc_ref[...]


def _mix_mlp(x, a, mod, g, w_o, w_in, w_out, rows_per_batch, tm, tf):
    t = x.shape[0]
    return pl.pallas_call(
        _mix_mlp_kernel,
        grid=(t // tm, D_FF // tf),
        in_specs=[pl.BlockSpec((tm, D_MODEL), lambda i, f: (i, 0)),
                  pl.BlockSpec((tm, D_MODEL), lambda i, f: (i, 0)),
                  pl.BlockSpec((1, MOD_PAD, D_MODEL), _mod_index(rows_per_batch, tm)),
                  pl.BlockSpec((1, D_MODEL), lambda i, f: (0, 0)),
                  pl.BlockSpec((D_MODEL, D_MODEL), lambda i, f: (0, 0)),
                  pl.BlockSpec((D_MODEL, tf), lambda i, f: (0, f)),
                  pl.BlockSpec((tf, D_MODEL), lambda i, f: (f, 0))],
        out_specs=pl.BlockSpec((tm, D_MODEL), lambda i, f: (i, 0)),
        out_shape=jax.ShapeDtypeStruct((t, D_MODEL), F32),
        scratch_shapes=[pltpu.VMEM((tm, D_MODEL), BF16), pltpu.VMEM((tm, D_MODEL), F32)],
        compiler_params=_cparams("parallel", "arbitrary"),
        name="mix_mlp",
    )(x, a, mod, g, w_o, w_in, w_out)


def _attn_kernel(*refs, kv_per_step, group, dk, dv, has_ctx):
    if has_ctx:
        q_ref, k_ref, v_ref, kc_ref, vc_ref, o_ref = refs
    else:
        q_ref, k_ref, v_ref, o_ref = refs

    def scores(n):
        j = n // group
        q = q_ref[:, n * dk:(n + 1) * dk]
        s = _dot_nt(q, k_ref[:, j * dk:(j + 1) * dk])
        sc = _dot_nt(q, kc_ref[:, j * dk:(j + 1) * dk]) if has_ctx else None
        return s, sc

    heads = kv_per_step * group
    nxt = scores(0)
    for n in range(heads):
        s, sc = nxt
        if n + 1 < heads:
            nxt = scores(n + 1)
        j = n // group
        m = jnp.max(s, axis=-1, keepdims=True)
        if has_ctx:
            m = jnp.maximum(m, jnp.max(sc, axis=-1, keepdims=True))
            pc = jnp.exp(sc - m)
        p = jnp.exp(s - m)
        l = jnp.sum(p, axis=-1, keepdims=True)
        o = _dot(p.astype(BF16), v_ref[:, j * dv:(j + 1) * dv])
        if has_ctx:
            l = l + jnp.sum(pc, axis=-1, keepdims=True)
            o = o + _dot(pc.astype(BF16), vc_ref[:, j * dv:(j + 1) * dv])
        o_ref[:, n * dv:(n + 1) * dv] = (o / l).astype(o_ref.dtype)


def _attention(q, k, v, ctx, *, batch, seq, heads_kv, kv_per_step, group, dk, dv, tq):
    nq = seq // tq
    hp = kv_per_step
    in_specs = [pl.BlockSpec((tq, hp * group * dk), lambda b, j, i: (b * nq + i, j)),
                pl.BlockSpec((seq, hp * dk), lambda b, j, i: (b, j)),
                pl.BlockSpec((seq, hp * dv), lambda b, j, i: (b, j))]
    args = [q, k, v]
    if ctx is not None:
        past = ctx[0].shape[0] // batch
        in_specs += [pl.BlockSpec((past, hp * dk), lambda b, j, i: (b, j)),
                     pl.BlockSpec((past, hp * dv), lambda b, j, i: (b, j))]
        args += list(ctx)
    return pl.pallas_call(
        functools.partial(_attn_kernel, kv_per_step=hp, group=group, dk=dk, dv=dv, has_ctx=ctx is not None),
        grid=(batch, heads_kv // hp, nq),
        in_specs=in_specs,
        out_specs=pl.BlockSpec((tq, hp * group * dv), lambda b, j, i: (b * nq + i, j)),
        out_shape=jax.ShapeDtypeStruct((batch * seq, heads_kv * group * dv), BF16),
        compiler_params=_cparams("parallel", "parallel", "parallel"),
        name="attention",
    )(*args)


def _swap_quarters(x, quarter):
    lanes = x.shape[-1]
    lane = lax.broadcasted_iota(jnp.int32, x.shape, x.ndim - 1)
    up = pltpu.roll(x, lanes - quarter, x.ndim - 1)
    down = pltpu.roll(x, quarter, x.ndim - 1)
    return jnp.where((lane // quarter) % 2 == 0, up, down)


def _rope(x, cos, sin, quarter):
    return x * cos + _swap_quarters(x, quarter) * sin


def _rope_tables(seq, width, pad_to):
    rows = seq // GRID_W
    row = jnp.repeat(jnp.arange(rows, dtype=F32), GRID_W)
    col = jnp.tile(jnp.arange(GRID_W, dtype=F32), rows)
    quarter = width // 4
    freqs = ROPE_THETA ** (-jnp.arange(quarter, dtype=F32) / quarter)
    ar = row[:, None] * freqs[None, :]
    ac = col[:, None] * freqs[None, :]
    cos = jnp.concatenate([jnp.cos(ar), jnp.cos(ar), jnp.cos(ac), jnp.cos(ac)], axis=-1)
    sin = jnp.concatenate([-jnp.sin(ar), jnp.sin(ar), -jnp.sin(ac), jnp.sin(ac)], axis=-1)
    pad = ((0, 0), (0, pad_to - width))
    return jnp.pad(cos, pad), jnp.pad(sin, pad)


def _gqa_prep_kernel(*refs, use_rope):
    if use_rope:
        p_ref, qg_ref, kg_ref, cos_ref, sin_ref, q_ref, k_ref, v_ref, kf_ref, vf_ref = refs
    else:
        p_ref, qg_ref, kg_ref, q_ref, k_ref, v_ref, kf_ref, vf_ref = refs
    for h in range(H_C + KVH_C):
        x = _rms(p_ref[:, h * HD_C:(h + 1) * HD_C], HD_C)
        x = x * (qg_ref[...] if h < H_C else kg_ref[...])
        if h >= H_C:
            kf_ref[:, (h - H_C) * HD_C:(h - H_C + 1) * HD_C] = x
        if use_rope:
            x = _rope(x, cos_ref[...], sin_ref[...], HD_C // 4)
        if h < H_C:
            q_ref[:, h * HD_C:(h + 1) * HD_C] = (x * HD_C ** -0.5).astype(BF16)
        else:
            k_ref[:, (h - H_C) * HD_C:(h - H_C + 1) * HD_C] = x.astype(BF16)
    vv = p_ref[:, (H_C + KVH_C) * HD_C:]
    vf_ref[...] = vv
    v_ref[...] = vv.astype(BF16)


def _gqa_prep(proj, q_g, k_g, tables, seq, tm):
    t = proj.shape[0]
    nq, nkv = H_C * HD_C, KVH_C * HD_C
    in_specs = [pl.BlockSpec((tm, nq + 2 * nkv), lambda i: (i, 0)),
                pl.BlockSpec((1, HD_C), lambda i: (0, 0)),
                pl.BlockSpec((1, HD_C), lambda i: (0, 0))]
    args = [proj, q_g, k_g]
    if tables is not None:
        per = seq // tm
        in_specs += [pl.BlockSpec((tm, HD_C), lambda i: (i % per, 0))] * 2
        args += list(tables)
    return pl.pallas_call(
        functools.partial(_gqa_prep_kernel, use_rope=tables is not None),
        grid=(t // tm,),
        in_specs=in_specs,
        out_specs=[pl.BlockSpec((tm, nq), lambda i: (i, 0))] + [pl.BlockSpec((tm, nkv), lambda i: (i, 0))] * 4,
        out_shape=[jax.ShapeDtypeStruct((t, nq), BF16), jax.ShapeDtypeStruct((t, nkv), BF16),
                   jax.ShapeDtypeStruct((t, nkv), BF16), jax.ShapeDtypeStruct((t, nkv), F32),
                   jax.ShapeDtypeStruct((t, nkv), F32)],
        compiler_params=_cparams("parallel"),
        name="gqa_prep",
    )(*args)


def _mla_keys_values(ckv, krope_slot, wukv_ref, knn_ref, k_ref, v_ref):
    kv = _dot(ckv.astype(BF16), wukv_ref[...])
    kr = krope_slot.astype(BF16)
    for h in range(H_B):
        base = h * (NOPE_B + V_B)
        kn = _rms(kv[:, base:base + NOPE_B], NOPE_B) * knn_ref[...]
        k_ref[:, h * SLOT_B:h * SLOT_B + NOPE_B] = kn.astype(BF16)
        k_ref[:, h * SLOT_B + NOPE_B:(h + 1) * SLOT_B] = kr
        v_ref[:, h * V_B:(h + 1) * V_B] = kv[:, base + NOPE_B:base + NOPE_B + V_B].astype(BF16)


def _mla_prep_kernel(*refs, use_rope):
    if use_rope:
        (p_ref, qlg_ref, kvg_ref, krg_ref, wuq_ref, wukv_ref, qnn_ref, qnr_ref, knn_ref, cos_ref, sin_ref,
         q_ref, k_ref, v_ref, ckv_ref, kr_ref) = refs
    else:
        (p_ref, qlg_ref, kvg_ref, krg_ref, wuq_ref, wukv_ref, qnn_ref, qnr_ref, knn_ref,
         q_ref, k_ref, v_ref, ckv_ref, kr_ref) = refs
    cq = _rms(p_ref[:, :Q_LORA], Q_LORA) * qlg_ref[...]
    ckv = _rms(p_ref[:, Q_LORA:Q_LORA + KV_LORA], KV_LORA) * kvg_ref[...]
    kr = _rms(p_ref[:, Q_LORA + KV_LORA:], ROPE_B) * krg_ref[...]
    ckv_ref[...] = ckv
    kr_ref[...] = kr
    q = _dot(cq.astype(BF16), wuq_ref[...])
    scale = (NOPE_B + ROPE_B) ** -0.5
    for h in range(H_B):
        qn = _rms(q[:, h * SLOT_B:h * SLOT_B + NOPE_B], NOPE_B) * qnn_ref[...]
        qr = _rms(q[:, h * SLOT_B + NOPE_B:(h + 1) * SLOT_B], ROPE_B) * qnr_ref[...]
        if use_rope:
            qr = _rope(qr, cos_ref[...], sin_ref[...], ROPE_B // 4)
        q_ref[:, h * SLOT_B:h * SLOT_B + NOPE_B] = (qn * scale).astype(BF16)
        q_ref[:, h * SLOT_B + NOPE_B:(h + 1) * SLOT_B] = (qr * scale).astype(BF16)
    if use_rope:
        kr = _rope(kr, cos_ref[...], sin_ref[...], ROPE_B // 4)
    _mla_keys_values(ckv, kr, wukv_ref, knn_ref, k_ref, v_ref)


def _mla_prep(proj, p, tables, seq, tm):
    t = proj.shape[0]
    full = lambda shape: pl.BlockSpec(shape, lambda i: (0, 0))
    in_specs = [pl.BlockSpec((tm, DOWN_PAD), lambda i: (i, 0)),
                full((1, Q_LORA)), full((1, KV_LORA)), full((1, LANE)),
                full((Q_LORA, H_B * SLOT_B)), full((KV_LORA, H_B * (NOPE_B + V_B))),
                full((1, NOPE_B)), full((1, LANE)), full((1, NOPE_B))]
    args = [proj, p["q_lat_g"], p["kv_lat_g"], p["kn_rope"], p["w_uq"], p["w_ukv"], p["qn_nope"], p["qn_rope"],
            p["kn_nope"]]
    if tables is not None:
        per = seq // tm
        in_specs += [pl.BlockSpec((tm, LANE), lambda i: (i % per, 0))] * 2
        args += list(tables)
    row = lambda n: pl.BlockSpec((tm, n), lambda i: (i, 0))
    return pl.pallas_call(
        functools.partial(_mla_prep_kernel, use_rope=tables is not None),
        grid=(t // tm,),
        in_specs=in_specs,
        out_specs=[row(H_B * SLOT_B), row(H_B * SLOT_B), row(H_B * V_B), row(KV_LORA), row(LANE)],
        out_shape=[jax.ShapeDtypeStruct((t, H_B * SLOT_B), BF16), jax.ShapeDtypeStruct((t, H_B * SLOT_B), BF16),
                   jax.ShapeDtypeStruct((t, H_B * V_B), BF16), jax.ShapeDtypeStruct((t, KV_LORA), F32),
                   jax.ShapeDtypeStruct((t, LANE), F32)],
        compiler_params=_cparams("parallel"),
        name="mla_prep",
    )(*args)


def _mla_ctx_kernel(ckv_ref, kr_ref, wukv_ref, knn_ref, k_ref, v_ref):
    _mla_keys_values(ckv_ref[...], kr_ref[...], wukv_ref, knn_ref, k_ref, v_ref)


def _mla_ctx(ckv, krope_slot, p, tm):
    t = ckv.shape[0]
    full = lambda shape: pl.BlockSpec(shape, lambda i: (0, 0))
    row = lambda n: pl.BlockSpec((tm, n), lambda i: (i, 0))
    return pl.pallas_call(
        _mla_ctx_kernel,
        grid=(t // tm,),
        in_specs=[row(KV_LORA), row(LANE), full((KV_LORA, H_B * (NOPE_B + V_B))), full((1, NOPE_B))],
        out_specs=[row(H_B * SLOT_B), row(H_B * V_B)],
        out_shape=[jax.ShapeDtypeStruct((t, H_B * SLOT_B), BF16), jax.ShapeDtypeStruct((t, H_B * V_B), BF16)],
        compiler_params=_cparams("parallel"),
        name="mla_ctx",
    )(ckv, krope_slot, p["w_ukv"], p["kn_nope"])


def _conv_silu(x, w):
    s = x.shape[0]
    row = lax.broadcasted_iota(jnp.int32, x.shape, 0)
    prev = jnp.where(row == 0, 0.0, pltpu.roll(x, 1, 0))
    nxt = jnp.where(row == s - 1, 0.0, pltpu.roll(x, s - 1, 0))
    y = prev * w[0:1, :] + x * w[1:2, :] + nxt * w[2:3, :]
    return y * jax.nn.sigmoid(y)


def _l2n(x):
    return x * lax.rsqrt(jnp.sum(x * x, axis=-1, keepdims=True) + EPS)


def _softplus(x):
    return jnp.maximum(x, 0.0) + jnp.log1p(jnp.exp(-jnp.abs(x)))


GDN_CHUNK_HEADS = 64
GDN_PACK = 4
GDN_UNROLL = 8


def _gdn_kernel(alog_ref, dtb_ref, q_ref, k_ref, v_ref, z_ref, cq_ref, ck_ref, cv_ref, gate_ref,
                s0f_ref, s0b_ref, on_ref, o_ref, sf_ref, sb_ref,
                qs, ks, vs, obuf, qeff, nmat, mneg, egt, st, *, seq, heads, unroll, pack):
    h0 = pl.program_id(1) * heads
    n_chunks = seq // CHUNK
    for hh in range(heads):
        cols = slice(hh * DK_A, (hh + 1) * DK_A)
        qs[hh] = _l2n(_conv_silu(q_ref[:, cols].astype(F32), cq_ref[:, cols])) * (DK_A ** -0.5)
        ks[hh] = _l2n(_conv_silu(k_ref[:, cols].astype(F32), ck_ref[:, cols]))
        vs[hh] = _conv_silu(v_ref[:, cols].astype(F32), cv_ref[:, cols])

    width = pack * CHUNK
    ii = lax.broadcasted_iota(jnp.int32, (CHUNK, CHUNK), 0)
    jj = lax.broadcasted_iota(jnp.int32, (CHUNK, CHUNK), 1)
    eye = ii == jj
    ip = lax.broadcasted_iota(jnp.int32, (CHUNK, width), 0)
    lane = lax.broadcasted_iota(jnp.int32, (CHUNK, width), 1)
    jp = lane % CHUNK
    in_blk = [(lane // CHUNK) == i for i in range(pack)]
    eye_p = ip == jp
    incl = (ii >= jj, ii <= jj)
    incl_p = (ip >= jp, ip <= jp)
    strict_p = (ip > jp, ip < jp)
    last = (CHUNK - 1, 0)
    pair_p = [(ip // 2) == (jp // 2)]
    pair_p += [((ip // (2 * s)) == (jp // (2 * s))) & ((ip // s) != (jp // s)) for s in (2, 4, 8, 16, 32)]
    slots = [(hh, d) for hh in range(heads) for d in range(2)]

    def pack_cols(cols):
        out = jnp.broadcast_to(cols[0], (CHUNK, width))
        for i in range(1, pack):
            out = jnp.where(in_blk[i], cols[i], out)
        return out

    def pack_diag(g):
        out = g[:CHUNK, :]
        for i in range(1, pack):
            out = jnp.where(in_blk[i], g[i * CHUNK:(i + 1) * CHUNK, :], out)
        return out

    def block_diag(x):
        return jnp.concatenate([jnp.where(in_blk[i], x, jnp.zeros_like(x)) for i in range(pack)], axis=0)
    neg_a = [-jnp.exp(jnp.full((1, CHUNK), alog_ref[d, h0 + hh], F32)) for hh, d in slots]
    dtb = [dtb_ref[d, h0 + hh] for hh, d in slots]

    def to_col(r):
        return jnp.sum(jnp.where(eye, jnp.broadcast_to(r, (CHUNK, CHUNK)), 0.0), axis=1, keepdims=True)

    def prepare(blk, carry):
        c0s = [blk * unroll + p * pack for p in range(unroll // pack)]
        offs = [pl.multiple_of(c0 * CHUNK, width) for c0 in c0s]
        tiles = [(hh, p) for hh in range(heads) for p in range(len(c0s))]
        qt = {t: qs[t[0], pl.ds(offs[t[1]], width), :] for t in tiles}
        kt = {t: ks[t[0], pl.ds(offs[t[1]], width), :] for t in tiles}
        vt = {t: vs[t[0], pl.ds(offs[t[1]], width), :] for t in tiles}
        ktb = {t: kt[t].astype(BF16) for t in tiles}
        kk = {t: pack_diag(_dot_nt(ktb[t], ktb[t])) for t in tiles}
        qk_raw = {t: pack_diag(_dot_nt(qt[t].astype(BF16), ktb[t])) for t in tiles}
        groups = [(hh, p, d) for hh, p in tiles for d in range(2)]
        a_mat, qk, beta_c, gcum_c, g_tot = [], [], [], [], []
        for hh, p, d in groups:
            slot = 2 * hh + d
            cols_beta, cols_gcum, tot = [], [], []
            for i in range(pack):
                g_raw = gate_ref[0, d * H_A + h0 + hh, pl.ds(c0s[p] + i, 1), :]
                b_raw = gate_ref[0, 2 * H_A + d * H_A + h0 + hh, pl.ds(c0s[p] + i, 1), :]
                g_row = neg_a[slot] * _softplus(g_raw + dtb[slot])
                gcum = jnp.sum(jnp.where(incl[d], jnp.broadcast_to(g_row, (CHUNK, CHUNK)), 0.0), axis=1,
                               keepdims=True)
                cols_gcum.append(gcum)
                cols_beta.append(to_col(jax.nn.sigmoid(b_raw)))
                tot.append(gcum[last[d]:last[d] + 1, :])
            gcum_p = pack_cols(cols_gcum)
            gcum_row = jnp.sum(jnp.where(eye_p, gcum_p, 0.0), axis=0, keepdims=True)
            decay = jnp.exp(jnp.where(incl_p[d], gcum_p - gcum_row, -jnp.inf))
            a_mat.append(jnp.where(strict_p[d], kk[hh, p] * pack_cols(cols_beta) * decay, 0.0))
            qk.append((qk_raw[hh, p] * decay).astype(BF16))
            beta_c.append(cols_beta)
            gcum_c.append(cols_gcum)
            g_tot.append(tot)
        tm = [jnp.where(eye_p, 1.0, -jnp.where(pair_p[0], a, 0.0)) for a in a_mat]
        for lvl in range(1, len(pair_p)):
            tb = [t.astype(BF16) for t in tm]
            x = [_dot(jnp.where(pair_p[lvl], a, 0.0).astype(BF16), block_diag(t)) for a, t in zip(a_mat, tb)]
            tm = [t - _dot(t16, block_diag(xi.astype(BF16))) for t, t16, xi in zip(tm, tb, x)]
        nm = [jnp.where(eye_p, 0.0, t).astype(BF16) for t in tm]
        rows = [slice(i * CHUNK, (i + 1) * CHUNK) for i in range(pack)]
        eg_c = [[jnp.exp(x) for x in cols] for cols in gcum_c]
        rhs = [jnp.concatenate(
            [jnp.concatenate([vt[hh, p][rows[i]] * beta_c[g][i], kt[hh, p][rows[i]] * (beta_c[g][i] * eg_c[g][i])],
                             axis=-1) for i in range(pack)], axis=0)
            for g, (hh, p, d) in enumerate(groups)]
        rhsb = [r.astype(BF16) for r in rhs]
        zero_p = jnp.zeros((CHUNK, width), BF16)
        sol = [jnp.concatenate([(rhs[g][rows[i]] + _dot(jnp.where(in_blk[i], nm[g], zero_p), rhsb[g])).astype(BF16)
                                for i in range(pack)], axis=0) for g in range(len(groups))]
        local = [jnp.concatenate([_dot(jnp.where(in_blk[i], qk[g], zero_p), sol[g]) for i in range(pack)], axis=0)
                 for g in range(len(groups))]
        k_dec = [[(kt[hh, p][rows[i]] * jnp.exp(g_tot[g][i] - gcum_c[g][i])).astype(BF16) for i in range(pack)]
                 for g, (hh, p, d) in enumerate(groups)]
        nm_mat = [[_dot_tn(k_dec[g][i], sol[g][rows[i]]) for i in range(pack)]
                  for g in range(len(groups))]
        for g, (hh, p, d) in enumerate(groups):
            slot = 2 * hh + d
            obuf[slot, pl.ds(offs[p], width), :] = local[g][:, :DV_A]
            q_eff = [qt[hh, p][rows[i]] * eg_c[g][i] - local[g][rows[i], DV_A:] for i in range(pack)]
            qeff[slot, pl.ds(offs[p], width), :] = jnp.concatenate(q_eff, axis=0).astype(BF16)
            for i in range(pack):
                nmat[slot, c0s[p] + i] = nm_mat[g][i][:, :DV_A]
                mneg[slot, c0s[p] + i] = (-nm_mat[g][i][:, DV_A:]).astype(BF16)
                egt[slot, c0s[p] + i] = jnp.broadcast_to(jnp.exp(g_tot[g][i]), (8, DV_A))
        return carry

    lax.fori_loop(0, n_chunks // unroll, prepare, 0)

    for hh, d in slots:
        st[2 * hh + d] = (s0f_ref, s0b_ref)[d][0, hh]

    def scan(ci, carry):
        states = [st[slot] for slot in range(len(slots))]
        sb16 = [s.astype(BF16) for s in states]
        cs = [ci if d == 0 else n_chunks - 1 - ci for hh, d in slots]
        for slot, c in enumerate(cs):
            st[slot] = states[slot] * egt[slot, c][0:1, :] + _dot(mneg[slot, c], sb16[slot]) + nmat[slot, c]
        for slot, c in enumerate(cs):
            off = pl.multiple_of(c * CHUNK, CHUNK)
            obuf[slot, pl.ds(off, CHUNK), :] += _dot(qeff[slot, pl.ds(off, CHUNK), :], sb16[slot])
        return carry

    lax.fori_loop(0, n_chunks, scan, 0)
    for hh, d in slots:
        (sf_ref, sb_ref)[d][0, hh] = st[2 * hh + d]

    for hh in range(heads):
        cols = slice(hh * DV_A, (hh + 1) * DV_A)
        z =z_ref[:, cols].astype(F32)
        o = _rms(obuf[2 * hh] + obuf[2 * hh + 1], DV_A) * on_ref[...] * (z * jax.nn.sigmoid(z))
        o_ref[:, cols] = o.astype(BF16)


def _gdn(proj, gates, conv_w, a_log, dt_bias, out_norm, s0f, s0b, batch, seq):
    n_chunks = seq // CHUNK
    unroll = math.gcd(n_chunks, GDN_UNROLL)
    pack = math.gcd(unroll, GDN_PACK if unroll > GDN_PACK else GDN_PACK // 2)
    hb = min(4, max(2, GDN_CHUNK_HEADS // n_chunks))
    groups, slots = H_A // hb, 2 * hb
    smem = pl.BlockSpec(memory_space=pltpu.SMEM)
    col = lambda off: pl.BlockSpec((seq, hb * DK_A), lambda b, h, off=off: (b, off + h))
    cw = lambda off: pl.BlockSpec((3, hb * DK_A), lambda b, h, off=off: (0, off + h))
    state = pl.BlockSpec((1, hb, DK_A, DV_A), lambda b, h: (b, h, 0, 0))
    st_shape = jax.ShapeDtypeStruct((batch, H_A, DK_A, DV_A), F32)
    return pl.pallas_call(
        functools.partial(_gdn_kernel, seq=seq, heads=hb, unroll=unroll, pack=pack),
        grid=(batch, groups),
        in_specs=[smem, smem, col(0), col(groups), col(2 * groups), col(3 * groups),
                  cw(0), cw(groups), cw(2 * groups),
                  pl.BlockSpec((1, 4 * H_A, n_chunks, CHUNK), lambda b, h: (b, 0, 0, 0)),
                  state, state, pl.BlockSpec((1, DV_A), lambda b, h: (0, 0))],
        out_specs=[pl.BlockSpec((seq, hb * DV_A), lambda b, h: (b, h)), state, state],
        out_shape=[jax.ShapeDtypeStruct((batch * seq, WV_A), BF16), st_shape, st_shape],
        scratch_shapes=[pltpu.VMEM((hb, seq, DK_A), F32), pltpu.VMEM((hb, seq, DK_A), F32),
                        pltpu.VMEM((hb, seq, DV_A), F32),
                        pltpu.VMEM((slots, seq, DV_A), F32), pltpu.VMEM((slots, seq, DK_A), BF16),
                        pltpu.VMEM((slots, n_chunks, DK_A, DV_A), F32),
                        pltpu.VMEM((slots, n_chunks, DK_A, DK_A), BF16),
                        pltpu.VMEM((slots, n_chunks, 8, DV_A), F32), pltpu.VMEM((slots, DK_A, DV_A), F32)],
        compiler_params=_cparams("parallel", "parallel"),
        name="gdn",
    )(a_log, dt_bias, proj, proj, proj, proj, conv_w, conv_w, conv_w, gates, s0f, s0b, out_norm)


def _pad_cols(w, n):
    return jnp.pad(w, ((0, 0), (0, n - w.shape[1])))


def kernel(x_prompt, x_sample, state_gdn_fwd, state_gdn_bwd, cache_mla_ckv, cache_mla_krope, cache_gqa_k, cache_gqa_v, c, c_ctx, norm_mix, norm_mlp, w_mod, b_mod, w_mlp_in, w_mlp_out, gdn_w_in, gdn_conv, gdn_a_log, gdn_dt_bias, gdn_out_norm, gdn_w_out, mla_w_down, mla_q_lat_norm, mla_kv_lat_norm, mla_w_uq, mla_w_ukv, mla_qn_nope, mla_qn_rope, mla_kn_nope, mla_kn_rope, mla_w_out, gqa_w_in, gqa_q_norm, gqa_k_norm, gqa_w_out):
    bp, sp, d = x_prompt.shape
    bs, ss, _ = x_sample.shape
    past = cache_mla_ckv.shape[2]
    xs = [x_prompt.reshape(bp * sp, d), x_sample.reshape(bs * ss, d)]
    batches, seqs = (bp, bs), (sp, ss)
    rows_per_batch = (None, ss)
    tms = (512, 1024)

    cond = jnp.zeros((MOD_ROWS, d), F32).at[:bs].set(c).at[CTX_ROW].set(c_ctx)
    mods = _adaln(cond, w_mod, b_mod).reshape(DEPTH, MOD_ROWS, N_MOD, d)
    mods = jnp.pad(mods, ((0, 0), (0, 0), (0, MOD_PAD - N_MOD), (0, 0)))

    gqa_tables = _rope_tables(ss, HD_C, HD_C)
    mla_tables = _rope_tables(ss, ROPE_B, LANE)

    new_gdn_f, new_gdn_b, new_ckv, new_kr, new_k, new_v = [], [], [], [], [], []
    for i in range(DEPTH):
        kind, j = i % N_MIXERS, i // N_MIXERS
        mod = mods[i]
        g_mix = norm_mix[i].reshape(1, d)
        g_mlp = norm_mlp[i].reshape(1, d)
        mixed = []
        if kind == 0:
            w_in = gdn_w_in[j]
            n_main = 2 * WK_A + 2 * WV_A
            w_main = w_in[:, :n_main].astype(BF16)
            w_gate = _pad_cols(w_in[:, n_main:], LANE).astype(BF16)
            w_o = gdn_w_out[j].astype(BF16)
            on = gdn_out_norm[j].reshape(1, DV_A)
            for gi in range(2):
                b, s, tm = batches[gi], seqs[gi], tms[gi]
                proj, gb = _normproj(xs[gi], mod, g_mix, w_main, rows_per_batch[gi], tm, 1024, BF16, w_gate)
                gates = gb[:, :4 * H_A].reshape(b, s // CHUNK, CHUNK, 4 * H_A).transpose(0, 3, 1, 2)
                if gi == 0:
                    s0f = s0b = jnp.zeros((b, H_A, DK_A, DV_A), F32)
                else:
                    s0f, s0b = state_gdn_fwd[:, j], state_gdn_bwd[:, j]
                o, s_f, s_b = _gdn(proj, gates, gdn_conv[j], gdn_a_log[j], gdn_dt_bias[j], on, s0f, s0b, b, s)
                if gi == 0:
                    new_gdn_f.append(s_f)
                    new_gdn_b.append(s_b)
                mixed.append((o, w_o))
        elif kind == 1:
            w_uq = mla_w_uq[j].reshape(Q_LORA, H_B, NOPE_B + ROPE_B)
            w_uq = jnp.pad(w_uq, ((0, 0), (0, 0), (0, SLOT_B - NOPE_B - ROPE_B))).reshape(Q_LORA, H_B * SLOT_B)
            p = dict(q_lat_g=mla_q_lat_norm[j].reshape(1, Q_LORA), kv_lat_g=mla_kv_lat_norm[j].reshape(1, KV_LORA),
                     kn_rope=_pad_cols(mla_kn_rope[j].reshape(1, ROPE_B), LANE),
                     qn_rope=_pad_cols(mla_qn_rope[j].reshape(1, ROPE_B), LANE),
                     qn_nope=mla_qn_nope[j].reshape(1, NOPE_B), kn_nope=mla_kn_nope[j].reshape(1, NOPE_B),
                     w_uq=w_uq.astype(BF16), w_ukv=mla_w_ukv[j].astype(BF16))
            w_down = _pad_cols(mla_w_down[j], DOWN_PAD).astype(BF16)
            w_o = mla_w_out[j].astype(BF16)
            for gi in range(2):
                b, s, tm = batches[gi], seqs[gi], tms[gi]
                proj = _normproj(xs[gi], mod, g_mix, w_down, rows_per_batch[gi], tm, DOWN_PAD)
                q, k, v, ckv, kr = _mla_prep(proj, p, mla_tables if gi == 1 else None, s, 512)
                ctx = None
                if gi == 0:
                    new_ckv.append(ckv.reshape(b, s, KV_LORA))
                    new_kr.append(kr[:, :ROPE_B].reshape(b, s, ROPE_B))
                else:
                    ckv_c = cache_mla_ckv[:, j].reshape(b * past, KV_LORA)
                    kr_c = _pad_cols(cache_mla_krope[:, j].reshape(b * past, ROPE_B), LANE)
                    ctx = _mla_ctx(ckv_c, kr_c, p, 512)
                o = _attention(q, k, v, ctx, batch=b, seq=s, heads_kv=H_B, kv_per_step=4, group=1, dk=SLOT_B,
                               dv=V_B, tq=256)
                mixed.append((o, w_o))
        else:
            w_in = gqa_w_in[j].astype(BF16)
            w_o = gqa_w_out[j].astype(BF16)
            q_g, k_g = gqa_q_norm[j].reshape(1, HD_C), gqa_k_norm[j].reshape(1, HD_C)
            for gi in range(2):
                b, s, tm = batches[gi], seqs[gi], tms[gi]
                proj = _normproj(xs[gi], mod, g_mix, w_in, rows_per_batch[gi], tm, 512)
                q, k, v, kf, vf = _gqa_prep(proj, q_g, k_g, gqa_tables if gi == 1 else None, s, 512)
                ctx = None
                if gi == 0:
                    new_k.append(kf.reshape(b, s, KVH_C, HD_C))
                    new_v.append(vf.reshape(b, s, KVH_C, HD_C))
                else:
                    ctx = (cache_gqa_k[:, j].reshape(b * past, KVH_C * HD_C).astype(BF16),
                           cache_gqa_v[:, j].reshape(b * past, KVH_C * HD_C).astype(BF16))
                o = _attention(q, k, v, ctx, batch=b, seq=s, heads_kv=KVH_C, kv_per_step=2, group=H_C // KVH_C,
                               dk=HD_C, dv=HD_C, tq=256)
                mixed.append((o, w_o))
        w_mi, w_mo = w_mlp_in[i].astype(BF16), w_mlp_out[i].astype(BF16)
        for gi in range(2):
            o, w_o = mixed[gi]
            xs[gi] = _mix_mlp(xs[gi], o, mod, g_mlp, w_o, w_mi, w_mo, rows_per_batch[gi], tms[gi], 1024)

    dt = x_prompt.dtype
    stack = lambda lst: jnp.stack(lst, axis=1).astype(dt)
    return (xs[0].reshape(bp, sp, d), xs[1].reshape(bs, ss, d), stack(new_gdn_f), stack(new_gdn_b),
            stack(new_ckv), stack(new_kr), stack(new_k), stack(new_v))
```

```python
import functools
import math

import jax
import jax.numpy as jnp
from jax import lax
from jax.experimental import pallas as pl
from jax.experimental.pallas import tpu as pltpu

F32 = jnp.float32
BF16 = jnp.bfloat16

D_MODEL = 1024
DEPTH = 4
GRID_W = 64
N_MIXERS = 3
D_FF = 4 * D_MODEL
N_MOD = 6
EPS = 1e-6
ROPE_THETA = 10000.0
H_A = 8
DK_A = 128
DV_A = 128
WK_A = H_A * DK_A
WV_A = H_A * DV_A
CHUNK = 64
H_B = 8
Q_LORA = 384
KV_LORA = 256
NOPE_B = 128
ROPE_B = 64
V_B = 128
SLOT_B = 256
DOWN_PAD = 768
H_C = 8
KVH_C = 2
HD_C = 128

LOG2_E = math.log2(math.e)
LANE = 128
MOD_ROWS = 16
CTX_ROW = 8
MOD_PAD = 8
VMEM_LIMIT = 56 * 1024 * 1024


def _cparams(*sem):
    return pltpu.CompilerParams(dimension_semantics=sem, vmem_limit_bytes=VMEM_LIMIT)


def _rms(x, n):
    return x * lax.rsqrt(jnp.sum(x * x, axis=-1, keepdims=True) * (1.0 / n) + EPS)


def _dot(a, b):
    return jnp.dot(a, b, preferred_element_type=F32)


def _dot_hi(a, b):
    return jnp.dot(a, b, preferred_element_type=F32, precision=lax.Precision.HIGHEST)


def _dot_nt(a, b):
    return lax.dot_general(a, b, (((1,), (1,)), ((), ())), preferred_element_type=F32)


def _dot_tn(a, b):
    return lax.dot_general(a, b, (((0,), (0,)), ((), ())), preferred_element_type=F32)


def _mod_index(rows_per_batch, tm):
    if rows_per_batch is None:
        return lambda t, *_: (CTX_ROW, 0, 0)
    return lambda t, *_: ((t * tm) // rows_per_batch, 0, 0)


def _adaln_kernel(c_ref, w_ref, b_ref, o_ref):
    c = c_ref[...]
    h = (c * jax.nn.sigmoid(c)).astype(BF16)
    o_ref[0] = _dot(h, w_ref[0].astype(BF16)) + b_ref[0]


def _adaln(cond, w_mod, b_mod):
    tn = 1024
    n = N_MOD * D_MODEL
    return pl.pallas_call(
        _adaln_kernel,
        grid=(DEPTH, n // tn),
        in_specs=[pl.BlockSpec((MOD_ROWS, D_MODEL), lambda i, j: (0, 0)),
                  pl.BlockSpec((1, D_MODEL, tn), lambda i, j: (i, 0, j)),
                  pl.BlockSpec((1, 1, tn), lambda i, j: (i, 0, j))],
        out_specs=pl.BlockSpec((1, MOD_ROWS, tn), lambda i, j: (i, 0, j)),
        out_shape=jax.ShapeDtypeStruct((DEPTH, MOD_ROWS, n), F32),
        compiler_params=_cparams("parallel", "parallel"),
        name="adaln",
    )(cond, w_mod, b_mod.reshape(DEPTH, 1, n))


def _modnorm(x, g, mod, shift_row, scale_row):
    y = _rms(x, D_MODEL) * g
    return y * (1.0 + mod[scale_row:scale_row + 1, :]) + mod[shift_row:shift_row + 1, :]


def _normproj_kernel(*refs, has_aux):
    if has_aux:
        x_ref, mod_ref, g_ref, w_ref, wa_ref, o_ref, oa_ref, h_ref = refs
    else:
        x_ref, mod_ref, g_ref, w_ref, o_ref, h_ref = refs

    @pl.when(pl.program_id(1) == 0)
    def _():
        h_ref[...] = _modnorm(x_ref[...], g_ref[...], mod_ref[0], 0, 1).astype(BF16)
        if has_aux:
            oa_ref[...] = _dot(h_ref[...], wa_ref[...])

    o_ref[...] = _dot(h_ref[...], w_ref[...]).astype(o_ref.dtype)


def _normproj(x, mod, g, w, rows_per_batch, tm, tn, out_dtype=F32, w_aux=None):
    t, n = x.shape[0], w.shape[1]
    in_specs = [pl.BlockSpec((tm, D_MODEL), lambda i, j: (i, 0)),
                pl.BlockSpec((1, MOD_PAD, D_MODEL), _mod_index(rows_per_batch, tm)),
                pl.BlockSpec((1, D_MODEL), lambda i, j: (0, 0)),
                pl.BlockSpec((D_MODEL, tn), lambda i, j: (0, j))]
    out_specs = [pl.BlockSpec((tm, tn), lambda i, j: (i, j))]
    out_shape = [jax.ShapeDtypeStruct((t, n), out_dtype)]
    args = [x, mod, g, w]
    if w_aux is not None:
        in_specs.append(pl.BlockSpec((D_MODEL, LANE), lambda i, j: (0, 0)))
        out_specs.append(pl.BlockSpec((tm, LANE), lambda i, j: (i, 0)))
        out_shape.append(jax.ShapeDtypeStruct((t, LANE), F32))
        args.append(w_aux)
    out = pl.pallas_call(
        functools.partial(_normproj_kernel, has_aux=w_aux is not None),
        grid=(t // tm, n // tn),
        in_specs=in_specs,
        out_specs=out_specs,
        out_shape=out_shape,
        scratch_shapes=[pltpu.VMEM((tm, D_MODEL), BF16)],
        compiler_params=_cparams("parallel", "arbitrary"),
        name="normproj",
    )(*args)
    return out if w_aux is not None else out[0]


def _mix_mlp_kernel(x_ref, a_ref, mod_ref, g_ref, wo_ref, win_ref, wout_ref, o_ref, h_ref, acc_ref):
    f = pl.program_id(1)

    @pl.when(f == 0)
    def _():
        o_ref[...] = x_ref[...] + mod_ref[0, 2:3, :] * _dot(a_ref[...], wo_ref[...])
        h_ref[...] = _modnorm(o_ref[...], g_ref[...], mod_ref[0], 3, 4).astype(BF16)
        acc_ref[...] = jnp.zeros_like(acc_ref)

    a = jnp.maximum(_dot(h_ref[...], win_ref[...]), 0.0)
    acc_ref[...] += _dot((a * a).astype(BF16), wout_ref[...])

    @pl.when(f == pl.num_programs(1) - 1)
    def _():
        o_ref[...] += mod_ref[0, 5:6, :] * acc_ref[...]


def _mix_mlp(x, a, mod, g, w_o, w_in, w_out, rows_per_batch, tm, tf):
    t = x.shape[0]
    return pl.pallas_call(
        _mix_mlp_kernel,
        grid=(t // tm, D_FF // tf),
        in_specs=[pl.BlockSpec((tm, D_MODEL), lambda i, f: (i, 0)),
                  pl.BlockSpec((tm, D_MODEL), lambda i, f: (i, 0)),
                  pl.BlockSpec((1, MOD_PAD, D_MODEL), _mod_index(rows_per_batch, tm)),
                  pl.BlockSpec((1, D_MODEL), lambda i, f: (0, 0)),
                  pl.BlockSpec((D_MODEL, D_MODEL), lambda i, f: (0, 0)),
                  pl.BlockSpec((D_MODEL, tf), lambda i, f: (0, f)),
                  pl.BlockSpec((tf, D_MODEL), lambda i, f: (f, 0))],
        out_specs=pl.BlockSpec((tm, D_MODEL), lambda i, f: (i, 0)),
        out_shape=jax.ShapeDtypeStruct((t, D_MODEL), F32),
        scratch_shapes=[pltpu.VMEM((tm, D_MODEL), BF16), pltpu.VMEM((tm, D_MODEL), F32)],
        compiler_params=_cparams("parallel", "arbitrary"),
        name="mix_mlp",
    )(x, a, mod, g, w_o, w_in, w_out)


def _attn_kernel(*refs, kv_per_step, group, dk, dv, has_ctx):
    if has_ctx:
        q_ref, k_ref, v_ref, kc_ref, vc_ref, o_ref = refs
    else:
        q_ref, k_ref, v_ref, o_ref = refs

    def scores(n):
        j = n // group
        q = q_ref[:, n * dk:(n + 1) * dk]
        s = _dot_nt(q, k_ref[:, j * dk:(j + 1) * dk])
        sc = _dot_nt(q, kc_ref[:, j * dk:(j + 1) * dk]) if has_ctx else None
        return s, sc

    heads = kv_per_step * group
    nxt = scores(0)
    for n in range(heads):
        s, sc = nxt
        if n + 1 < heads:
            nxt = scores(n + 1)
        j = n // group
        m = jnp.max(s, axis=-1, keepdims=True)
        if has_ctx:
            m = jnp.maximum(m, jnp.max(sc, axis=-1, keepdims=True))
            pc = jnp.exp2(sc - m)
        p = jnp.exp2(s - m)
        l = jnp.sum(p, axis=-1, keepdims=True)
        o = _dot(p.astype(BF16), v_ref[:, j * dv:(j + 1) * dv])
        if has_ctx:
            l = l + jnp.sum(pc, axis=-1, keepdims=True)
            o = o + _dot(pc.astype(BF16), vc_ref[:, j * dv:(j + 1) * dv])
        o_ref[:, n * dv:(n + 1) * dv] = (o / l).astype(o_ref.dtype)


def _attention(q, k, v, ctx, *, batch, seq, heads_kv, kv_per_step, group, dk, dv, tq):
    nq = seq // tq
    hp = kv_per_step
    in_specs = [pl.BlockSpec((tq, hp * group * dk), lambda b, j, i: (b * nq + i, j)),
                pl.BlockSpec((seq, hp * dk), lambda b, j, i: (b, j)),
                pl.BlockSpec((seq, hp * dv), lambda b, j, i: (b, j))]
    args = [q, k, v]
    if ctx is not None:
        past = ctx[0].shape[0] // batch
        in_specs += [pl.BlockSpec((past, hp * dk), lambda b, j, i: (b, j)),
                     pl.BlockSpec((past, hp * dv), lambda b, j, i: (b, j))]
        args += list(ctx)
    return pl.pallas_call(
        functools.partial(_attn_kernel, kv_per_step=hp, group=group, dk=dk, dv=dv, has_ctx=ctx is not None),
        grid=(batch, heads_kv // hp, nq),
        in_specs=in_specs,
        out_specs=pl.BlockSpec((tq, hp * group * dv), lambda b, j, i: (b * nq + i, j)),
        out_shape=jax.ShapeDtypeStruct((batch * seq, heads_kv * group * dv), BF16),
        compiler_params=_cparams("parallel", "parallel", "parallel"),
        name="attention",
    )(*args)


def _swap_quarters(x, quarter):
    lanes = x.shape[-1]
    lane = lax.broadcasted_iota(jnp.int32, x.shape, x.ndim - 1)
    up = pltpu.roll(x, lanes - quarter, x.ndim - 1)
    down = pltpu.roll(x, quarter, x.ndim - 1)
    return jnp.where((lane // quarter) % 2 == 0, up, down)


def _rope(x, cos, sin, quarter):
    return x * cos + _swap_quarters(x, quarter) * sin


def _rope_tables(seq, width, pad_to):
    rows = seq // GRID_W
    row = jnp.repeat(jnp.arange(rows, dtype=F32), GRID_W)
    col = jnp.tile(jnp.arange(GRID_W, dtype=F32), rows)
    quarter = width // 4
    freqs = ROPE_THETA ** (-jnp.arange(quarter, dtype=F32) / quarter)
    ar = row[:, None] * freqs[None, :]
    ac = col[:, None] * freqs[None, :]
    cos = jnp.concatenate([jnp.cos(ar), jnp.cos(ar), jnp.cos(ac), jnp.cos(ac)], axis=-1)
    sin = jnp.concatenate([-jnp.sin(ar), jnp.sin(ar), -jnp.sin(ac), jnp.sin(ac)], axis=-1)
    pad = ((0, 0), (0, pad_to - width))
    return jnp.pad(cos, pad), jnp.pad(sin, pad)


def _gqa_prep_kernel(*refs, use_rope):
    if use_rope:
        p_ref, qg_ref, kg_ref, cos_ref, sin_ref, q_ref, k_ref, v_ref, kf_ref, vf_ref = refs
    else:
        p_ref, qg_ref, kg_ref, q_ref, k_ref, v_ref, kf_ref, vf_ref = refs
    for h in range(H_C + KVH_C):
        x = _rms(p_ref[:, h * HD_C:(h + 1) * HD_C], HD_C)
        x = x * (qg_ref[...] if h < H_C else kg_ref[...])
        if h >= H_C:
            kf_ref[:, (h - H_C) * HD_C:(h - H_C + 1) * HD_C] = x
        if use_rope:
            x = _rope(x, cos_ref[...], sin_ref[...], HD_C // 4)
        if h < H_C:
            q_ref[:, h * HD_C:(h + 1) * HD_C] = (x * (LOG2_E * HD_C ** -0.5)).astype(BF16)
        else:
            k_ref[:, (h - H_C) * HD_C:(h - H_C + 1) * HD_C] = x.astype(BF16)
    vv = p_ref[:, (H_C + KVH_C) * HD_C:]
    vf_ref[...] = vv
    v_ref[...] = vv.astype(BF16)


def _gqa_prep(proj, q_g, k_g, tables, seq, tm):
    t = proj.shape[0]
    nq, nkv = H_C * HD_C, KVH_C * HD_C
    in_specs = [pl.BlockSpec((tm, nq + 2 * nkv), lambda i: (i, 0)),
                pl.BlockSpec((1, HD_C), lambda i: (0, 0)),
                pl.BlockSpec((1, HD_C), lambda i: (0, 0))]
    args = [proj, q_g, k_g]
    if tables is not None:
        per = seq // tm
        in_specs += [pl.BlockSpec((tm, HD_C), lambda i: (i % per, 0))] * 2
        args += list(tables)
    return pl.pallas_call(
        functools.partial(_gqa_prep_kernel, use_rope=tables is not None),
        grid=(t // tm,),
        in_specs=in_specs,
        out_specs=[pl.BlockSpec((tm, nq), lambda i: (i, 0))] + [pl.BlockSpec((tm, nkv), lambda i: (i, 0))] * 4,
        out_shape=[jax.ShapeDtypeStruct((t, nq), BF16), jax.ShapeDtypeStruct((t, nkv), BF16),
                   jax.ShapeDtypeStruct((t, nkv), BF16), jax.ShapeDtypeStruct((t, nkv), F32),
                   jax.ShapeDtypeStruct((t, nkv), F32)],
        compiler_params=_cparams("parallel"),
        name="gqa_prep",
    )(*args)


def _mla_keys_values(ckv, krope_slot, wukv_ref, knn_ref, k_ref, v_ref):
    kv = _dot(ckv.astype(BF16), wukv_ref[...])
    kr = krope_slot.astype(BF16)
    for h in range(H_B):
        base = h * (NOPE_B + V_B)
        kn = _rms(kv[:, base:base + NOPE_B], NOPE_B) * knn_ref[...]
        k_ref[:, h * SLOT_B:h * SLOT_B + NOPE_B] = kn.astype(BF16)
        k_ref[:, h * SLOT_B + NOPE_B:(h + 1) * SLOT_B] = kr
        v_ref[:, h * V_B:(h + 1) * V_B] = kv[:, base + NOPE_B:base + NOPE_B + V_B].astype(BF16)


def _mla_prep_kernel(*refs, use_rope):
    if use_rope:
        (p_ref, qlg_ref, kvg_ref, krg_ref, wuq_ref, wukv_ref, qnn_ref, qnr_ref, knn_ref, cos_ref, sin_ref,
         q_ref, k_ref, v_ref, ckv_ref, kr_ref) = refs
    else:
        (p_ref, qlg_ref, kvg_ref, krg_ref, wuq_ref, wukv_ref, qnn_ref, qnr_ref, knn_ref,
         q_ref, k_ref, v_ref, ckv_ref, kr_ref) = refs
    cq = _rms(p_ref[:, :Q_LORA], Q_LORA) * qlg_ref[...]
    ckv = _rms(p_ref[:, Q_LORA:Q_LORA + KV_LORA], KV_LORA) * kvg_ref[...]
    kr = _rms(p_ref[:, Q_LORA + KV_LORA:], ROPE_B) * krg_ref[...]
    ckv_ref[...] = ckv
    kr_ref[...] = kr
    q = _dot(cq.astype(BF16), wuq_ref[...])
    scale = LOG2_E * (NOPE_B + ROPE_B) ** -0.5
    for h in range(H_B):
        qn = _rms(q[:, h * SLOT_B:h * SLOT_B + NOPE_B], NOPE_B) * qnn_ref[...]
        qr = _rms(q[:, h * SLOT_B + NOPE_B:(h + 1) * SLOT_B], ROPE_B) * qnr_ref[...]
        if use_rope:
            qr = _rope(qr, cos_ref[...], sin_ref[...], ROPE_B // 4)
        q_ref[:, h * SLOT_B:h * SLOT_B + NOPE_B] = (qn * scale).astype(BF16)
        q_ref[:, h * SLOT_B + NOPE_B:(h + 1) * SLOT_B] = (qr * scale).astype(BF16)
    if use_rope:
        kr = _rope(kr, cos_ref[...], sin_ref[...], ROPE_B // 4)
    _mla_keys_values(ckv, kr, wukv_ref, knn_ref, k_ref, v_ref)


def _mla_prep(proj, p, tables, seq, tm):
    t = proj.shape[0]
    full = lambda shape: pl.BlockSpec(shape, lambda i: (0, 0))
    in_specs = [pl.BlockSpec((tm, DOWN_PAD), lambda i: (i, 0)),
                full((1, Q_LORA)), full((1, KV_LORA)), full((1, LANE)),
                full((Q_LORA, H_B * SLOT_B)), full((KV_LORA, H_B * (NOPE_B + V_B))),
                full((1, NOPE_B)), full((1, LANE)), full((1, NOPE_B))]
    args = [proj, p["q_lat_g"], p["kv_lat_g"], p["kn_rope"], p["w_uq"], p["w_ukv"], p["qn_nope"], p["qn_rope"],
            p["kn_nope"]]
    if tables is not None:
        per = seq // tm
        in_specs += [pl.BlockSpec((tm, LANE), lambda i: (i % per, 0))] * 2
        args += list(tables)
    row = lambda n: pl.BlockSpec((tm, n), lambda i: (i, 0))
    return pl.pallas_call(
        functools.partial(_mla_prep_kernel, use_rope=tables is not None),
        grid=(t // tm,),
        in_specs=in_specs,
        out_specs=[row(H_B * SLOT_B), row(H_B * SLOT_B), row(H_B * V_B), row(KV_LORA), row(LANE)],
        out_shape=[jax.ShapeDtypeStruct((t, H_B * SLOT_B), BF16), jax.ShapeDtypeStruct((t, H_B * SLOT_B), BF16),
                   jax.ShapeDtypeStruct((t, H_B * V_B), BF16), jax.ShapeDtypeStruct((t, KV_LORA), F32),
                   jax.ShapeDtypeStruct((t, LANE), F32)],
        compiler_params=_cparams("parallel"),
        name="mla_prep",
    )(*args)


def _mla_ctx_kernel(ckv_ref, kr_ref, wukv_ref, knn_ref, k_ref, v_ref):
    _mla_keys_values(ckv_ref[...], kr_ref[...], wukv_ref, knn_ref, k_ref, v_ref)


def _mla_ctx(ckv, krope_slot, p, tm):
    t = ckv.shape[0]
    full = lambda shape: pl.BlockSpec(shape, lambda i: (0, 0))
    row = lambda n: pl.BlockSpec((tm, n), lambda i: (i, 0))
    return pl.pallas_call(
        _mla_ctx_kernel,
        grid=(t // tm,),
        in_specs=[row(KV_LORA), row(LANE), full((KV_LORA, H_B * (NOPE_B + V_B))), full((1, NOPE_B))],
        out_specs=[row(H_B * SLOT_B), row(H_B * V_B)],
        out_shape=[jax.ShapeDtypeStruct((t, H_B * SLOT_B), BF16), jax.ShapeDtypeStruct((t, H_B * V_B), BF16)],
        compiler_params=_cparams("parallel"),
        name="mla_ctx",
    )(ckv, krope_slot, p["w_ukv"], p["kn_nope"])


def _silu(y):
    h = 0.5 * y
    return h * jnp.tanh(h) + h


def _conv_silu(x, w):
    s = x.shape[0]
    row = lax.broadcasted_iota(jnp.int32, x.shape, 0)
    prev = jnp.where(row == 0, 0.0, pltpu.roll(x, 1, 0))
    nxt = jnp.where(row == s - 1, 0.0, pltpu.roll(x, s - 1, 0))
    y = prev * w[0:1, :] + x * w[1:2, :] + nxt * w[2:3, :]
    return _silu(y)


def _l2n(x):
    return x * lax.rsqrt(jnp.sum(x * x, axis=-1, keepdims=True) + EPS)


def _softplus(x):
    return jnp.maximum(x, 0.0) + jnp.log1p(jnp.exp(-jnp.abs(x)))


GDN_CHUNK_HEADS = 64
GDN_PACK = 4
GDN_UNROLL = 8


def _gdn_kernel(alog_ref, dtb_ref, q_ref, k_ref, v_ref, z_ref, cq_ref, ck_ref, cv_ref, gate_ref,
                s0f_ref, s0b_ref, on_ref, o_ref, sf_ref, sb_ref,
                qs, ks, vs, obuf, qeff, nmat, mneg, egt, st, *, seq, heads, unroll, pack):
    h0 = pl.program_id(1) * heads
    n_chunks = seq // CHUNK
    for hh in range(heads):
        cols = slice(hh * DK_A, (hh + 1) * DK_A)
        qs[hh] = _l2n(_conv_silu(q_ref[:, cols].astype(F32), cq_ref[:, cols])) * (DK_A ** -0.5)
        ks[hh] = _l2n(_conv_silu(k_ref[:, cols].astype(F32), ck_ref[:, cols]))
        vs[hh] = _conv_silu(v_ref[:, cols].astype(F32), cv_ref[:, cols])

    width = pack * CHUNK
    ii = lax.broadcasted_iota(jnp.int32, (CHUNK, CHUNK), 0)
    jj = lax.broadcasted_iota(jnp.int32, (CHUNK, CHUNK), 1)
    eye = ii == jj
    ip = lax.broadcasted_iota(jnp.int32, (CHUNK, width), 0)
    lane = lax.broadcasted_iota(jnp.int32, (CHUNK, width), 1)
    jp = lane % CHUNK
    in_blk = [(lane // CHUNK) == i for i in range(pack)]
    eye_p = ip == jp
    incl = (ii >= jj, ii <= jj)
    incl_p = (ip >= jp, ip <= jp)
    strict_p = (ip > jp, ip < jp)
    last = (CHUNK - 1, 0)
    pair_p = [(ip // 2) == (jp // 2)]
    pair_p += [((ip // (2 * s)) == (jp // (2 * s))) & ((ip // s) != (jp // s)) for s in (2, 4, 8, 16, 32)]
    slots = [(hh, d) for hh in range(heads) for d in range(2)]

    def pack_cols(cols):
        out = jnp.broadcast_to(cols[0], (CHUNK, width))
        for i in range(1, pack):
            out = jnp.where(in_blk[i], cols[i], out)
        return out

    def pack_diag(g):
        out = g[:CHUNK, :]
        for i in range(1, pack):
            out = jnp.where(in_blk[i], g[i * CHUNK:(i + 1) * CHUNK, :], out)
        return out

    def block_diag(x):
        return jnp.concatenate([jnp.where(in_blk[i], x, jnp.zeros_like(x)) for i in range(pack)], axis=0)
    neg_a = [-jnp.exp(jnp.full((1, CHUNK), alog_ref[d, h0 + hh], F32)) for hh, d in slots]
    dtb = [dtb_ref[d, h0 + hh] for hh, d in slots]

    def to_col(r):
        return jnp.sum(jnp.where(eye, jnp.broadcast_to(r, (CHUNK, CHUNK)), 0.0), axis=1, keepdims=True)

    def prepare(blk, carry):
        c0s = [blk * unroll + p * pack for p in range(unroll // pack)]
        offs = [pl.multiple_of(c0 * CHUNK, width) for c0 in c0s]
        tiles = [(hh, p) for hh in range(heads) for p in range(len(c0s))]
        qt = {t: qs[t[0], pl.ds(offs[t[1]], width), :] for t in tiles}
        kt = {t: ks[t[0], pl.ds(offs[t[1]], width), :] for t in tiles}
        vt = {t: vs[t[0], pl.ds(offs[t[1]], width), :] for t in tiles}
        ktb = {t: kt[t].astype(BF16) for t in tiles}
        kk = {t: pack_diag(_dot_nt(ktb[t], ktb[t])) for t in tiles}
        qk_raw = {t: pack_diag(_dot_nt(qt[t].astype(BF16), ktb[t])) for t in tiles}
        groups = [(hh, p, d) for hh, p in tiles for d in range(2)]
        a_mat, qk, beta_c, gcum_c, g_tot = [], [], [], [], []
        for hh, p, d in groups:
            slot = 2 * hh + d
            cols_beta, cols_gcum, tot = [], [], []
            for i in range(pack):
                g_raw = gate_ref[0, d * H_A + h0 + hh, pl.ds(c0s[p] + i, 1), :]
                b_raw = gate_ref[0, 2 * H_A + d * H_A + h0 + hh, pl.ds(c0s[p] + i, 1), :]
                g_row = neg_a[slot] * _softplus(g_raw + dtb[slot])
                gcum = jnp.sum(jnp.where(incl[d], jnp.broadcast_to(g_row, (CHUNK, CHUNK)), 0.0), axis=1,
                               keepdims=True)
                cols_gcum.append(gcum)
                cols_beta.append(to_col(jax.nn.sigmoid(b_raw)))
                tot.append(gcum[last[d]:last[d] + 1, :])
            gcum_p = pack_cols(cols_gcum)
            gcum_row = jnp.sum(jnp.where(eye_p, gcum_p, 0.0), axis=0, keepdims=True)
            decay = jnp.exp(jnp.where(incl_p[d], gcum_p - gcum_row, -jnp.inf))
            a_mat.append(jnp.where(strict_p[d], kk[hh, p] * pack_cols(cols_beta) * decay, 0.0))
            qk.append((qk_raw[hh, p] * decay).astype(BF16))
            beta_c.append(cols_beta)
            gcum_c.append(cols_gcum)
            g_tot.append(tot)
        tm = [jnp.where(eye_p, 1.0, -jnp.where(pair_p[0], a, 0.0)) for a in a_mat]
        for lvl in range(1, len(pair_p)):
            tb = [t.astype(BF16) for t in tm]
            x = [_dot(jnp.where(pair_p[lvl], a, 0.0).astype(BF16), block_diag(t)) for a, t in zip(a_mat, tb)]
            tm = [t - _dot(t16, block_diag(xi.astype(BF16))) for t, t16, xi in zip(tm, tb, x)]
        nm = [jnp.where(eye_p, 0.0, t).astype(BF16) for t in tm]
        rows = [slice(i * CHUNK, (i + 1) * CHUNK) for i in range(pack)]
        eg_c = [[jnp.exp(x) for x in cols] for cols in gcum_c]
        rhs = [jnp.concatenate(
            [jnp.concatenate([vt[hh, p][rows[i]] * beta_c[g][i], kt[hh, p][rows[i]] * (beta_c[g][i] * eg_c[g][i])],
                             axis=-1) for i in range(pack)], axis=0)
            for g, (hh, p, d) in enumerate(groups)]
        rhsb = [r.astype(BF16) for r in rhs]
        zero_p = jnp.zeros((CHUNK, width), BF16)
        sol = [jnp.concatenate([(rhs[g][rows[i]] + _dot(jnp.where(in_blk[i], nm[g], zero_p), rhsb[g])).astype(BF16)
                                for i in range(pack)], axis=0) for g in range(len(groups))]
        local = [jnp.concatenate([_dot(jnp.where(in_blk[i], qk[g], zero_p), sol[g]) for i in range(pack)], axis=0)
                 for g in range(len(groups))]
        k_dec = [[(kt[hh, p][rows[i]] * jnp.exp(g_tot[g][i] - gcum_c[g][i])).astype(BF16) for i in range(pack)]
                 for g, (hh, p, d) in enumerate(groups)]
        nm_mat = [[_dot_tn(k_dec[g][i], sol[g][rows[i]]) for i in range(pack)]
                  for g in range(len(groups))]
        for g, (hh, p, d) in enumerate(groups):
            slot = 2 * hh + d
            obuf[slot, pl.ds(offs[p], width), :] = local[g][:, :DV_A]
            q_eff = [qt[hh, p][rows[i]] * eg_c[g][i] - local[g][rows[i], DV_A:] for i in range(pack)]
            qeff[slot, pl.ds(offs[p], width), :] = jnp.concatenate(q_eff, axis=0).astype(BF16)
            for i in range(pack):
                nmat[slot, c0s[p] + i] = nm_mat[g][i][:, :DV_A]
                mneg[slot, c0s[p] + i] = (-nm_mat[g][i][:, DV_A:]).astype(BF16)
                egt[slot, c0s[p] + i] = jnp.broadcast_to(jnp.exp(g_tot[g][i]), (8, DV_A))
        return carry

    lax.fori_loop(0, n_chunks // unroll, prepare, 0)

    for hh, d in slots:
        st[2 * hh + d] = (s0f_ref, s0b_ref)[d][0, hh]

    def scan(ci, carry):
        states = [st[slot] for slot in range(len(slots))]
        sb16 = [s.astype(BF16) for s in states]
        cs = [ci if d == 0 else n_chunks - 1 - ci for hh, d in slots]
        for slot, c in enumerate(cs):
            st[slot] = states[slot] * egt[slot, c][0:1, :] + _dot(mneg[slot, c], sb16[slot]) + nmat[slot, c]
        for slot, c in enumerate(cs):
            off = pl.multiple_of(c * CHUNK, CHUNK)
            obuf[slot, pl.ds(off, CHUNK), :] += _dot(qeff[slot, pl.ds(off, CHUNK), :], sb16[slot])
        return carry

    lax.fori_loop(0, n_chunks, scan, 0, unroll=4)
    for hh, d in slots:
        (sf_ref, sb_ref)[d][0, hh] = st[2 * hh + d]

    for hh in range(heads):
        cols = slice(hh * DV_A, (hh + 1) * DV_A)
        z =z_ref[:, cols].astype(F32)
        o = _rms(obuf[2 * hh] + obuf[2 * hh + 1], DV_A) * on_ref[...] * _silu(z)
        o_ref[:, cols] = o.astype(BF16)


def _gdn(proj, gates, conv_w, a_log, dt_bias, out_norm, s0f, s0b, batch, seq):
    n_chunks = seq // CHUNK
    unroll = math.gcd(n_chunks, GDN_UNROLL)
    pack = math.gcd(unroll, GDN_PACK if unroll > GDN_PACK else GDN_PACK // 2)
    hb = min(4, max(2, GDN_CHUNK_HEADS // n_chunks))
    groups, slots = H_A // hb, 2 * hb
    smem = pl.BlockSpec(memory_space=pltpu.SMEM)
    col = lambda off: pl.BlockSpec((seq, hb * DK_A), lambda b, h, off=off: (b, off + h))
    cw = lambda off: pl.BlockSpec((3, hb * DK_A), lambda b, h, off=off: (0, off + h))
    state = pl.BlockSpec((1, hb, DK_A, DV_A), lambda b, h: (b, h, 0, 0))
    st_shape = jax.ShapeDtypeStruct((batch, H_A, DK_A, DV_A), F32)
    return pl.pallas_call(
        functools.partial(_gdn_kernel, seq=seq, heads=hb, unroll=unroll, pack=pack),
        grid=(batch, groups),
        in_specs=[smem, smem, col(0), col(groups), col(2 * groups), col(3 * groups),
                  cw(0), cw(groups), cw(2 * groups),
                  pl.BlockSpec((1, 4 * H_A, n_chunks, CHUNK), lambda b, h: (b, 0, 0, 0)),
                  state, state, pl.BlockSpec((1, DV_A), lambda b, h: (0, 0))],
        out_specs=[pl.BlockSpec((seq, hb * DV_A), lambda b, h: (b, h)), state, state],
        out_shape=[jax.ShapeDtypeStruct((batch * seq, WV_A), BF16), st_shape, st_shape],
        scratch_shapes=[pltpu.VMEM((hb, seq, DK_A), F32), pltpu.VMEM((hb, seq, DK_A), F32),
                        pltpu.VMEM((hb, seq, DV_A), F32),
                        pltpu.VMEM((slots, seq, DV_A), F32), pltpu.VMEM((slots, seq, DK_A), BF16),
                        pltpu.VMEM((slots, n_chunks, DK_A, DV_A), F32),
                        pltpu.VMEM((slots, n_chunks, DK_A, DK_A), BF16),
                        pltpu.VMEM((slots, n_chunks, 8, DV_A), F32), pltpu.VMEM((slots, DK_A, DV_A), F32)],
        compiler_params=_cparams("parallel", "parallel"),
        name="gdn",
    )(a_log, dt_bias, proj, proj, proj, proj, conv_w, conv_w, conv_w, gates, s0f, s0b, out_norm)


def _pad_cols(w, n):
    return jnp.pad(w, ((0, 0), (0, n - w.shape[1])))


def kernel(x_prompt, x_sample, state_gdn_fwd, state_gdn_bwd, cache_mla_ckv, cache_mla_krope, cache_gqa_k, cache_gqa_v, c, c_ctx, norm_mix, norm_mlp, w_mod, b_mod, w_mlp_in, w_mlp_out, gdn_w_in, gdn_conv, gdn_a_log, gdn_dt_bias, gdn_out_norm, gdn_w_out, mla_w_down, mla_q_lat_norm, mla_kv_lat_norm, mla_w_uq, mla_w_ukv, mla_qn_nope, mla_qn_rope, mla_kn_nope, mla_kn_rope, mla_w_out, gqa_w_in, gqa_q_norm, gqa_k_norm, gqa_w_out):
    bp, sp, d = x_prompt.shape
    bs, ss, _ = x_sample.shape
    past = cache_mla_ckv.shape[2]
    xs = [x_prompt.reshape(bp * sp, d), x_sample.reshape(bs * ss, d)]
    batches, seqs = (bp, bs), (sp, ss)
    rows_per_batch = (None, ss)
    tms = (512, 1024)

    cond = jnp.zeros((MOD_ROWS, d), F32).at[:bs].set(c).at[CTX_ROW].set(c_ctx)
    mods = _adaln(cond, w_mod, b_mod).reshape(DEPTH, MOD_ROWS, N_MOD, d)
    mods = jnp.pad(mods, ((0, 0), (0, 0), (0, MOD_PAD - N_MOD), (0, 0)))

    gqa_tables = _rope_tables(ss, HD_C, HD_C)
    mla_tables = _rope_tables(ss, ROPE_B, LANE)

    new_gdn_f, new_gdn_b, new_ckv, new_kr, new_k, new_v = [], [], [], [], [], []
    for i in range(DEPTH):
        kind, j = i % N_MIXERS, i // N_MIXERS
        mod = mods[i]
        g_mix = norm_mix[i].reshape(1, d)
        g_mlp = norm_mlp[i].reshape(1, d)
        mixed = []
        if kind == 0:
            w_in = gdn_w_in[j]
            n_main = 2 * WK_A + 2 * WV_A
            w_main = w_in[:, :n_main].astype(BF16)
            w_gate = _pad_cols(w_in[:, n_main:], LANE).astype(BF16)
            w_o = gdn_w_out[j].astype(BF16)
            on = gdn_out_norm[j].reshape(1, DV_A)
            for gi in range(2):
                b, s, tm = batches[gi], seqs[gi], tms[gi]
                proj, gb = _normproj(xs[gi], mod, g_mix, w_main, rows_per_batch[gi], tm, 1024, BF16, w_gate)
                gates = gb[:, :4 * H_A].reshape(b, s // CHUNK, CHUNK, 4 * H_A).transpose(0, 3, 1, 2)
                if gi == 0:
                    s0f = s0b = jnp.zeros((b, H_A, DK_A, DV_A), F32)
                else:
                    s0f, s0b = state_gdn_fwd[:, j], state_gdn_bwd[:, j]
                o, s_f, s_b = _gdn(proj, gates, gdn_conv[j], gdn_a_log[j], gdn_dt_bias[j], on, s0f, s0b, b, s)
                if gi == 0:
                    new_gdn_f.append(s_f)
                    new_gdn_b.append(s_b)
                mixed.append((o, w_o))
        elif kind == 1:
            w_uq = mla_w_uq[j].reshape(Q_LORA, H_B, NOPE_B + ROPE_B)
            w_uq = jnp.pad(w_uq, ((0, 0), (0, 0), (0, SLOT_B - NOPE_B - ROPE_B))).reshape(Q_LORA, H_B * SLOT_B)
            p = dict(q_lat_g=mla_q_lat_norm[j].reshape(1, Q_LORA), kv_lat_g=mla_kv_lat_norm[j].reshape(1, KV_LORA),
                     kn_rope=_pad_cols(mla_kn_rope[j].reshape(1, ROPE_B), LANE),
                     qn_rope=_pad_cols(mla_qn_rope[j].reshape(1, ROPE_B), LANE),
                     qn_nope=mla_qn_nope[j].reshape(1, NOPE_B), kn_nope=mla_kn_nope[j].reshape(1, NOPE_B),
                     w_uq=w_uq.astype(BF16), w_ukv=mla_w_ukv[j].astype(BF16))
            w_down = _pad_cols(mla_w_down[j], DOWN_PAD).astype(BF16)
            w_o = mla_w_out[j].astype(BF16)
            for gi in range(2):
                b, s, tm = batches[gi], seqs[gi], tms[gi]
                proj = _normproj(xs[gi], mod, g_mix, w_down, rows_per_batch[gi], tm, DOWN_PAD)
                q, k, v, ckv, kr = _mla_prep(proj, p, mla_tables if gi == 1 else None, s, 512)
                ctx = None
                if gi == 0:
                    new_ckv.append(ckv.reshape(b, s, KV_LORA))
                    new_kr.append(kr[:, :ROPE_B].reshape(b, s, ROPE_B))
                else:
                    ckv_c = cache_mla_ckv[:, j].reshape(b * past, KV_LORA)
                    kr_c = _pad_cols(cache_mla_krope[:, j].reshape(b * past, ROPE_B), LANE)
                    ctx = _mla_ctx(ckv_c, kr_c, p, 512)
                o = _attention(q, k, v, ctx, batch=b, seq=s, heads_kv=H_B, kv_per_step=4, group=1, dk=SLOT_B,
                               dv=V_B, tq=256)
                mixed.append((o, w_o))
        else:
            w_in = gqa_w_in[j].astype(BF16)
            w_o = gqa_w_out[j].astype(BF16)
            q_g, k_g = gqa_q_norm[j].reshape(1, HD_C), gqa_k_norm[j].reshape(1, HD_C)
            for gi in range(2):
                b, s, tm = batches[gi], seqs[gi], tms[gi]
                proj = _normproj(xs[gi], mod, g_mix, w_in, rows_per_batch[gi], tm, 512)
                q, k, v, kf, vf = _gqa_prep(proj, q_g, k_g, gqa_tables if gi == 1 else None, s, 512)
                ctx = None
                if gi == 0:
                    new_k.append(kf.reshape(b, s, KVH_C, HD_C))
                    new_v.append(vf.reshape(b, s, KVH_C, HD_C))
                else:
                    ctx = (cache_gqa_k[:, j].reshape(b * past, KVH_C * HD_C).astype(BF16),
                           cache_gqa_v[:, j].reshape(b * past, KVH_C * HD_C).astype(BF16))
                o = _attention(q, k, v, ctx, batch=b, seq=s, heads_kv=KVH_C, kv_per_step=2, group=H_C // KVH_C,
                               dk=HD_C, dv=HD_C, tq=256)
                mixed.append((o, w_o))
        w_mi, w_mo = w_mlp_in[i].astype(BF16), w_mlp_out[i].astype(BF16)
        for gi in range(2):
            o, w_o = mixed[gi]
            xs[gi] = _mix_mlp(xs[gi], o, mod, g_mlp, w_o, w_mi, w_mo, rows_per_batch[gi], tms[gi], 1024)

    dt = x_prompt.dtype
    stack = lambda lst: jnp.stack(lst, axis=1).astype(dt)
    return (xs[0].reshape(bp, sp, d), xs[1].reshape(bs, ss, d), stack(new_gdn_f), stack(new_gdn_b),
            stack(new_ckv), stack(new_kr), stack(new_k), stack(new_v))
```

```python
import functools
import math

import jax
import jax.numpy as jnp
from jax import lax
from jax.experimental import pallas as pl
from jax.experimental.pallas import tpu as pltpu

F32 = jnp.float32
BF16 = jnp.bfloat16

D_MODEL = 1024
DEPTH = 4
GRID_W = 64
N_MIXERS = 3
D_FF = 4 * D_MODEL
N_MOD = 6
EPS = 1e-6
ROPE_THETA = 10000.0
H_A = 8
DK_A = 128
DV_A = 128
WK_A = H_A * DK_A
WV_A = H_A * DV_A
CHUNK = 64
H_B = 8
Q_LORA = 384
KV_LORA = 256
NOPE_B = 128
ROPE_B = 64
V_B = 128
SLOT_B = 256
DOWN_PAD = 768
H_C = 8
KVH_C = 2
HD_C = 128

LOG2_E = math.log2(math.e)
LANE = 128
MOD_ROWS = 16
CTX_ROW = 8
MOD_PAD = 8
VMEM_LIMIT = 56 * 1024 * 1024


def _cparams(*sem):
    return pltpu.CompilerParams(dimension_semantics=sem, vmem_limit_bytes=VMEM_LIMIT)


def _rms(x, n):
    return x * lax.rsqrt(jnp.sum(x * x, axis=-1, keepdims=True) * (1.0 / n) + EPS)


def _dot(a, b):
    return jnp.dot(a, b, preferred_element_type=F32)


def _dot_hi(a, b):
    return jnp.dot(a, b, preferred_element_type=F32, precision=lax.Precision.HIGHEST)


def _dot_nt(a, b):
    return lax.dot_general(a, b, (((1,), (1,)), ((), ())), preferred_element_type=F32)


def _dot_tn(a, b):
    return lax.dot_general(a, b, (((0,), (0,)), ((), ())), preferred_element_type=F32)


def _mod_index(rows_per_batch, tm):
    if rows_per_batch is None:
        return lambda t, *_: (CTX_ROW, 0, 0)
    return lambda t, *_: ((t * tm) // rows_per_batch, 0, 0)


def _adaln_kernel(c_ref, w_ref, b_ref, o_ref):
    c = c_ref[...]
    h = (c * jax.nn.sigmoid(c)).astype(BF16)
    o_ref[0] = _dot(h, w_ref[0].astype(BF16)) + b_ref[0]


def _adaln(cond, w_mod, b_mod):
    tn = 1024
    n = N_MOD * D_MODEL
    return pl.pallas_call(
        _adaln_kernel,
        grid=(DEPTH, n // tn),
        in_specs=[pl.BlockSpec((MOD_ROWS, D_MODEL), lambda i, j: (0, 0)),
                  pl.BlockSpec((1, D_MODEL, tn), lambda i, j: (i, 0, j)),
                  pl.BlockSpec((1, 1, tn), lambda i, j: (i, 0, j))],
        out_specs=pl.BlockSpec((1, MOD_ROWS, tn), lambda i, j: (i, 0, j)),
        out_shape=jax.ShapeDtypeStruct((DEPTH, MOD_ROWS, n), F32),
        compiler_params=_cparams("parallel", "parallel"),
        name="adaln",
    )(cond, w_mod, b_mod.reshape(DEPTH, 1, n))


def _modnorm(x, g, mod, shift_row, scale_row):
    y = _rms(x, D_MODEL) * g
    return y * (1.0 + mod[scale_row:scale_row + 1, :]) + mod[shift_row:shift_row + 1, :]


def _normproj_kernel(*refs, has_aux):
    if has_aux:
        x_ref, mod_ref, g_ref, w_ref, wa_ref, o_ref, oa_ref, h_ref = refs
    else:
        x_ref, mod_ref, g_ref, w_ref, o_ref, h_ref = refs

    j = pl.program_id(1)
    tn = o_ref.shape[1]

    @pl.when(j == 0)
    def _():
        h_ref[...] = _modnorm(x_ref[...], g_ref[...], mod_ref[0], 0, 1).astype(BF16)
        if has_aux:
            oa_ref[...] = _dot(h_ref[...], wa_ref[...])

    w = w_ref[:, pl.ds(pl.multiple_of(j * tn, tn), tn)]
    o_ref[...] = _dot(h_ref[...], w).astype(o_ref.dtype)


def _normproj(x, mod, g, w, rows_per_batch, tm, tn, out_dtype=F32, w_aux=None):
    t, n = x.shape[0], w.shape[1]
    in_specs = [pl.BlockSpec((tm, D_MODEL), lambda i, j: (i, 0)),
                pl.BlockSpec((1, MOD_PAD, D_MODEL), _mod_index(rows_per_batch, tm)),
                pl.BlockSpec((1, D_MODEL), lambda i, j: (0, 0)),
                pl.BlockSpec((D_MODEL, n), lambda i, j: (0, 0))]
    out_specs = [pl.BlockSpec((tm, tn), lambda i, j: (i, j))]
    out_shape = [jax.ShapeDtypeStruct((t, n), out_dtype)]
    args = [x, mod, g, w]
    if w_aux is not None:
        in_specs.append(pl.BlockSpec((D_MODEL, LANE), lambda i, j: (0, 0)))
        out_specs.append(pl.BlockSpec((tm, LANE), lambda i, j: (i, 0)))
        out_shape.append(jax.ShapeDtypeStruct((t, LANE), F32))
        args.append(w_aux)
    out = pl.pallas_call(
        functools.partial(_normproj_kernel, has_aux=w_aux is not None),
        grid=(t // tm, n // tn),
        in_specs=in_specs,
        out_specs=out_specs,
        out_shape=out_shape,
        scratch_shapes=[pltpu.VMEM((tm, D_MODEL), BF16)],
        compiler_params=_cparams("parallel", "arbitrary"),
        name="normproj",
    )(*args)
    return out if w_aux is not None else out[0]


def _mix_mlp_kernel(x_ref, a_ref, mod_ref, g_ref, wo_ref, win_ref, wout_ref, o_ref, h_ref, acc_ref):
    f = pl.program_id(1)

    @pl.when(f == 0)
    def _():
        o_ref[...] = x_ref[...] + mod_ref[0, 2:3, :] * _dot(a_ref[...], wo_ref[...])
        h_ref[...] = _modnorm(o_ref[...], g_ref[...], mod_ref[0], 3, 4).astype(BF16)
        acc_ref[...] = jnp.zeros_like(acc_ref)

    a = jnp.maximum(_dot(h_ref[...], win_ref[...]), 0.0)
    acc_ref[...] += _dot((a * a).astype(BF16), wout_ref[...])

    @pl.when(f == pl.num_programs(1) - 1)
    def _():
        o_ref[...] += mod_ref[0, 5:6, :] * acc_ref[...]


def _mix_mlp(x, a, mod, g, w_o, w_in, w_out, rows_per_batch, tm, tf):
    t = x.shape[0]
    return pl.pallas_call(
        _mix_mlp_kernel,
        grid=(t // tm, D_FF // tf),
        in_specs=[pl.BlockSpec((tm, D_MODEL), lambda i, f: (i, 0)),
                  pl.BlockSpec((tm, D_MODEL), lambda i, f: (i, 0)),
                  pl.BlockSpec((1, MOD_PAD, D_MODEL), _mod_index(rows_per_batch, tm)),
                  pl.BlockSpec((1, D_MODEL), lambda i, f: (0, 0)),
                  pl.BlockSpec((D_MODEL, D_MODEL), lambda i, f: (0, 0)),
                  pl.BlockSpec((D_MODEL, tf), lambda i, f: (0, f)),
                  pl.BlockSpec((tf, D_MODEL), lambda i, f: (f, 0))],
        out_specs=pl.BlockSpec((tm, D_MODEL), lambda i, f: (i, 0)),
        out_shape=jax.ShapeDtypeStruct((t, D_MODEL), F32),
        scratch_shapes=[pltpu.VMEM((tm, D_MODEL), BF16), pltpu.VMEM((tm, D_MODEL), F32)],
        compiler_params=_cparams("parallel", "arbitrary"),
        name="mix_mlp",
    )(x, a, mod, g, w_o, w_in, w_out)


def _attn_kernel(*refs, kv_per_step, group, dk, dv, has_ctx):
    if has_ctx:
        q_ref, k_ref, v_ref, kc_ref, vc_ref, o_ref = refs
    else:
        q_ref, k_ref, v_ref, o_ref = refs

    def scores(n):
        j = n // group
        q = q_ref[:, n * dk:(n + 1) * dk]
        s = _dot_nt(q, k_ref[:, j * dk:(j + 1) * dk])
        sc = _dot_nt(q, kc_ref[:, j * dk:(j + 1) * dk]) if has_ctx else None
        return s, sc

    heads = kv_per_step * group
    nxt = scores(0)
    for n in range(heads):
        s, sc = nxt
        if n + 1 < heads:
            nxt = scores(n + 1)
        j = n // group
        m = jnp.max(s, axis=-1, keepdims=True)
        if has_ctx:
            m = jnp.maximum(m, jnp.max(sc, axis=-1, keepdims=True))
            pc = jnp.exp2(sc - m)
        p = jnp.exp2(s - m)
        l = jnp.sum(p, axis=-1, keepdims=True)
        o = _dot(p.astype(BF16), v_ref[:, j * dv:(j + 1) * dv])
        if has_ctx:
            l = l + jnp.sum(pc, axis=-1, keepdims=True)
            o = o + _dot(pc.astype(BF16), vc_ref[:, j * dv:(j + 1) * dv])
        o_ref[:, n * dv:(n + 1) * dv] = (o / l).astype(o_ref.dtype)


def _attention(q, k, v, ctx, *, batch, seq, heads_kv, kv_per_step, group, dk, dv, tq):
    nq = seq // tq
    hp = kv_per_step
    in_specs = [pl.BlockSpec((tq, hp * group * dk), lambda b, j, i: (b * nq + i, j)),
                pl.BlockSpec((seq, hp * dk), lambda b, j, i: (b, j)),
                pl.BlockSpec((seq, hp * dv), lambda b, j, i: (b, j))]
    args = [q, k, v]
    if ctx is not None:
        past = ctx[0].shape[0] // batch
        in_specs += [pl.BlockSpec((past, hp * dk), lambda b, j, i: (b, j)),
                     pl.BlockSpec((past, hp * dv), lambda b, j, i: (b, j))]
        args += list(ctx)
    return pl.pallas_call(
        functools.partial(_attn_kernel, kv_per_step=hp, group=group, dk=dk, dv=dv, has_ctx=ctx is not None),
        grid=(batch, heads_kv // hp, nq),
        in_specs=in_specs,
        out_specs=pl.BlockSpec((tq, hp * group * dv), lambda b, j, i: (b * nq + i, j)),
        out_shape=jax.ShapeDtypeStruct((batch * seq, heads_kv * group * dv), BF16),
        compiler_params=_cparams("parallel", "parallel", "parallel"),
        name="attention",
    )(*args)


def _swap_quarters(x, quarter):
    lanes = x.shape[-1]
    lane = lax.broadcasted_iota(jnp.int32, x.shape, x.ndim - 1)
    up = pltpu.roll(x, lanes - quarter, x.ndim - 1)
    down = pltpu.roll(x, quarter, x.ndim - 1)
    return jnp.where((lane // quarter) % 2 == 0, up, down)


def _rope(x, cos, sin, quarter):
    return x * cos + _swap_quarters(x, quarter) * sin


def _rope_tables(seq, width, pad_to):
    rows = seq // GRID_W
    row = jnp.repeat(jnp.arange(rows, dtype=F32), GRID_W)
    col = jnp.tile(jnp.arange(GRID_W, dtype=F32), rows)
    quarter = width // 4
    freqs = ROPE_THETA ** (-jnp.arange(quarter, dtype=F32) / quarter)
    ar = row[:, None] * freqs[None, :]
    ac = col[:, None] * freqs[None, :]
    cos = jnp.concatenate([jnp.cos(ar), jnp.cos(ar), jnp.cos(ac), jnp.cos(ac)], axis=-1)
    sin = jnp.concatenate([-jnp.sin(ar), jnp.sin(ar), -jnp.sin(ac), jnp.sin(ac)], axis=-1)
    pad = ((0, 0), (0, pad_to - width))
    return jnp.pad(cos, pad), jnp.pad(sin, pad)


def _gqa_prep_kernel(*refs, use_rope):
    if use_rope:
        x_ref, mod_ref, g_ref, w_ref, qg_ref, kg_ref, cos_ref, sin_ref, q_ref, k_ref, v_ref, kf_ref, vf_ref = refs
    else:
        x_ref, mod_ref, g_ref, w_ref, qg_ref, kg_ref, q_ref, k_ref, v_ref, kf_ref, vf_ref = refs
    proj = _dot(_modnorm(x_ref[...], g_ref[...], mod_ref[0], 0, 1).astype(BF16), w_ref[...])
    for h in range(H_C + KVH_C):
        x = _rms(proj[:, h * HD_C:(h + 1) * HD_C], HD_C)
        x = x * (qg_ref[...] if h < H_C else kg_ref[...])
        if h >= H_C:
            kf_ref[:, (h - H_C) * HD_C:(h - H_C + 1) * HD_C] = x
        if use_rope:
            x = _rope(x, cos_ref[...], sin_ref[...], HD_C // 4)
        if h < H_C:
            q_ref[:, h * HD_C:(h + 1) * HD_C] = (x * (LOG2_E * HD_C ** -0.5)).astype(BF16)
        else:
            k_ref[:, (h - H_C) * HD_C:(h - H_C + 1) * HD_C] = x.astype(BF16)
    vv = proj[:, (H_C + KVH_C) * HD_C:]
    vf_ref[...] = vv
    v_ref[...] = vv.astype(BF16)


def _gqa_prep(x, mod, g, w, q_g, k_g, tables, rows_per_batch, seq, tm):
    t = x.shape[0]
    nq, nkv = H_C * HD_C, KVH_C * HD_C
    in_specs = [pl.BlockSpec((tm, D_MODEL), lambda i: (i, 0)),
                pl.BlockSpec((1, MOD_PAD, D_MODEL), _mod_index(rows_per_batch, tm)),
                pl.BlockSpec((1, D_MODEL), lambda i: (0, 0)),
                pl.BlockSpec((D_MODEL, nq + 2 * nkv), lambda i: (0, 0)),
                pl.BlockSpec((1, HD_C), lambda i: (0, 0)),
                pl.BlockSpec((1, HD_C), lambda i: (0, 0))]
    args = [x, mod, g, w, q_g, k_g]
    if tables is not None:
        per = seq // tm
        in_specs += [pl.BlockSpec((tm, HD_C), lambda i: (i % per, 0))] * 2
        args += list(tables)
    return pl.pallas_call(
        functools.partial(_gqa_prep_kernel, use_rope=tables is not None),
        grid=(t // tm,),
        in_specs=in_specs,
        out_specs=[pl.BlockSpec((tm, nq), lambda i: (i, 0))] + [pl.BlockSpec((tm, nkv), lambda i: (i, 0))] * 4,
        out_shape=[jax.ShapeDtypeStruct((t, nq), BF16), jax.ShapeDtypeStruct((t, nkv), BF16),
                   jax.ShapeDtypeStruct((t, nkv), BF16), jax.ShapeDtypeStruct((t, nkv), F32),
                   jax.ShapeDtypeStruct((t, nkv), F32)],
        compiler_params=_cparams("parallel"),
        name="gqa_prep",
    )(*args)


def _mla_keys_values(ckv, krope_slot, wukv_ref, knn_ref, k_ref, v_ref):
    kv = _dot(ckv.astype(BF16), wukv_ref[...])
    kr = krope_slot.astype(BF16)
    for h in range(H_B):
        base = h * (NOPE_B + V_B)
        kn = _rms(kv[:, base:base + NOPE_B], NOPE_B) * knn_ref[...]
        k_ref[:, h * SLOT_B:h * SLOT_B + NOPE_B] = kn.astype(BF16)
        k_ref[:, h * SLOT_B + NOPE_B:(h + 1) * SLOT_B] = kr
        v_ref[:, h * V_B:(h + 1) * V_B] = kv[:, base + NOPE_B:base + NOPE_B + V_B].astype(BF16)


def _mla_prep_kernel(*refs, use_rope):
    if use_rope:
        (p_ref, qlg_ref, kvg_ref, krg_ref, wuq_ref, wukv_ref, qnn_ref, qnr_ref, knn_ref, cos_ref, sin_ref,
         q_ref, k_ref, v_ref, ckv_ref, kr_ref) = refs
    else:
        (p_ref, qlg_ref, kvg_ref, krg_ref, wuq_ref, wukv_ref, qnn_ref, qnr_ref, knn_ref,
         q_ref, k_ref, v_ref, ckv_ref, kr_ref) = refs
    cq = _rms(p_ref[:, :Q_LORA], Q_LORA) * qlg_ref[...]
    ckv = _rms(p_ref[:, Q_LORA:Q_LORA + KV_LORA], KV_LORA) * kvg_ref[...]
    kr = _rms(p_ref[:, Q_LORA + KV_LORA:], ROPE_B) * krg_ref[...]
    ckv_ref[...] = ckv
    kr_ref[...] = kr
    q = _dot(cq.astype(BF16), wuq_ref[...])
    scale = LOG2_E * (NOPE_B + ROPE_B) ** -0.5
    for h in range(H_B):
        qn = _rms(q[:, h * SLOT_B:h * SLOT_B + NOPE_B], NOPE_B) * qnn_ref[...]
        qr = _rms(q[:, h * SLOT_B + NOPE_B:(h + 1) * SLOT_B], ROPE_B) * qnr_ref[...]
        if use_rope:
            qr = _rope(qr, cos_ref[...], sin_ref[...], ROPE_B // 4)
        q_ref[:, h * SLOT_B:h * SLOT_B + NOPE_B] = (qn * scale).astype(BF16)
        q_ref[:, h * SLOT_B + NOPE_B:(h + 1) * SLOT_B] = (qr * scale).astype(BF16)
    if use_rope:
        kr = _rope(kr, cos_ref[...], sin_ref[...], ROPE_B // 4)
    _mla_keys_values(ckv, kr, wukv_ref, knn_ref, k_ref, v_ref)


def _mla_prep(proj, p, tables, seq, tm):
    t = proj.shape[0]
    full = lambda shape: pl.BlockSpec(shape, lambda i: (0, 0))
    in_specs = [pl.BlockSpec((tm, DOWN_PAD), lambda i: (i, 0)),
                full((1, Q_LORA)), full((1, KV_LORA)), full((1, LANE)),
                full((Q_LORA, H_B * SLOT_B)), full((KV_LORA, H_B * (NOPE_B + V_B))),
                full((1, NOPE_B)), full((1, LANE)), full((1, NOPE_B))]
    args = [proj, p["q_lat_g"], p["kv_lat_g"], p["kn_rope"], p["w_uq"], p["w_ukv"], p["qn_nope"], p["qn_rope"],
            p["kn_nope"]]
    if tables is not None:
        per = seq // tm
        in_specs += [pl.BlockSpec((tm, LANE), lambda i: (i % per, 0))] * 2
        args += list(tables)
    row = lambda n: pl.BlockSpec((tm, n), lambda i: (i, 0))
    return pl.pallas_call(
        functools.partial(_mla_prep_kernel, use_rope=tables is not None),
        grid=(t // tm,),
        in_specs=in_specs,
        out_specs=[row(H_B * SLOT_B), row(H_B * SLOT_B), row(H_B * V_B), row(KV_LORA), row(LANE)],
        out_shape=[jax.ShapeDtypeStruct((t, H_B * SLOT_B), BF16), jax.ShapeDtypeStruct((t, H_B * SLOT_B), BF16),
                   jax.ShapeDtypeStruct((t, H_B * V_B), BF16), jax.ShapeDtypeStruct((t, KV_LORA), F32),
                   jax.ShapeDtypeStruct((t, LANE), F32)],
        compiler_params=_cparams("parallel"),
        name="mla_prep",
    )(*args)


def _mla_ctx_kernel(ckv_ref, kr_ref, wukv_ref, knn_ref, k_ref, v_ref):
    _mla_keys_values(ckv_ref[...], kr_ref[...], wukv_ref, knn_ref, k_ref, v_ref)


def _mla_ctx(ckv, krope_slot, p, tm):
    t = ckv.shape[0]
    full = lambda shape: pl.BlockSpec(shape, lambda i: (0, 0))
    row = lambda n: pl.BlockSpec((tm, n), lambda i: (i, 0))
    return pl.pallas_call(
        _mla_ctx_kernel,
        grid=(t // tm,),
        in_specs=[row(KV_LORA), row(LANE), full((KV_LORA, H_B * (NOPE_B + V_B))), full((1, NOPE_B))],
        out_specs=[row(H_B * SLOT_B), row(H_B * V_B)],
        out_shape=[jax.ShapeDtypeStruct((t, H_B * SLOT_B), BF16), jax.ShapeDtypeStruct((t, H_B * V_B), BF16)],
        compiler_params=_cparams("parallel"),
        name="mla_ctx",
    )(ckv, krope_slot, p["w_ukv"], p["kn_nope"])


def _silu(y):
    h = 0.5 * y
    return h * jnp.tanh(h) + h


def _conv_silu(x, w):
    s = x.shape[0]
    row = lax.broadcasted_iota(jnp.int32, x.shape, 0)
    prev = jnp.where(row == 0, 0.0, pltpu.roll(x, 1, 0))
    nxt = jnp.where(row == s - 1, 0.0, pltpu.roll(x, s - 1, 0))
    y = prev * w[0:1, :] + x * w[1:2, :] + nxt * w[2:3, :]
    return _silu(y)


def _l2n(x):
    return x * lax.rsqrt(jnp.sum(x * x, axis=-1, keepdims=True) + EPS)


def _softplus(x):
    return jnp.maximum(x, 0.0) + jnp.log1p(jnp.exp(-jnp.abs(x)))


GDN_CHUNK_HEADS = 64
GDN_PACK = 4
GDN_UNROLL = 8


def _gdn_kernel(alog_ref, dtb_ref, q_ref, k_ref, v_ref, z_ref, cq_ref, ck_ref, cv_ref, gate_ref,
                s0f_ref, s0b_ref, on_ref, o_ref, sf_ref, sb_ref,
                qs, ks, vs, obuf, qeff, nmat, mneg, egt, st, *, seq, heads, unroll, pack):
    h0 = pl.program_id(1) * heads
    n_chunks = seq // CHUNK
    for hh in range(heads):
        cols = slice(hh * DK_A, (hh + 1) * DK_A)
        qs[hh] = _l2n(_conv_silu(q_ref[:, cols].astype(F32), cq_ref[:, cols])) * (DK_A ** -0.5)
        ks[hh] = _l2n(_conv_silu(k_ref[:, cols].astype(F32), ck_ref[:, cols]))
        vs[hh] = _conv_silu(v_ref[:, cols].astype(F32), cv_ref[:, cols])

    width = pack * CHUNK
    ii = lax.broadcasted_iota(jnp.int32, (CHUNK, CHUNK), 0)
    jj = lax.broadcasted_iota(jnp.int32, (CHUNK, CHUNK), 1)
    eye = ii == jj
    ip = lax.broadcasted_iota(jnp.int32, (CHUNK, width), 0)
    lane = lax.broadcasted_iota(jnp.int32, (CHUNK, width), 1)
    jp = lane % CHUNK
    in_blk = [(lane // CHUNK) == i for i in range(pack)]
    eye_p = ip == jp
    incl = (ii >= jj, ii <= jj)
    incl_p = (ip >= jp, ip <= jp)
    strict_p = (ip > jp, ip < jp)
    last = (CHUNK - 1, 0)
    pair_p = [(ip // 2) == (jp // 2)]
    pair_p += [((ip // (2 * s)) == (jp // (2 * s))) & ((ip // s) != (jp // s)) for s in (2, 4, 8, 16, 32)]
    slots = [(hh, d) for hh in range(heads) for d in range(2)]

    def pack_cols(cols):
        out = jnp.broadcast_to(cols[0], (CHUNK, width))
        for i in range(1, pack):
            out = jnp.where(in_blk[i], cols[i], out)
        return out

    def pack_diag(g):
        out = g[:CHUNK, :]
        for i in range(1, pack):
            out = jnp.where(in_blk[i], g[i * CHUNK:(i + 1) * CHUNK, :], out)
        return out

    def block_diag(x):
        return jnp.concatenate([jnp.where(in_blk[i], x, jnp.zeros_like(x)) for i in range(pack)], axis=0)
    neg_a = [-jnp.exp(jnp.full((1, CHUNK), alog_ref[d, h0 + hh], F32)) for hh, d in slots]
    dtb = [dtb_ref[d, h0 + hh] for hh, d in slots]

    def to_col(r):
        return jnp.sum(jnp.where(eye, jnp.broadcast_to(r, (CHUNK, CHUNK)), 0.0), axis=1, keepdims=True)

    def prepare(blk, carry):
        c0s = [blk * unroll + p * pack for p in range(unroll // pack)]
        offs = [pl.multiple_of(c0 * CHUNK, width) for c0 in c0s]
        tiles = [(hh, p) for hh in range(heads) for p in range(len(c0s))]
        qt = {t: qs[t[0], pl.ds(offs[t[1]], width), :] for t in tiles}
        kt = {t: ks[t[0], pl.ds(offs[t[1]], width), :] for t in tiles}
        vt = {t: vs[t[0], pl.ds(offs[t[1]], width), :] for t in tiles}
        ktb = {t: kt[t].astype(BF16) for t in tiles}
        kk = {t: pack_diag(_dot_nt(ktb[t], ktb[t])) for t in tiles}
        qk_raw = {t: pack_diag(_dot_nt(qt[t].astype(BF16), ktb[t])) for t in tiles}
        groups = [(hh, p, d) for hh, p in tiles for d in range(2)]
        a_mat, qk, beta_c, gcum_c, g_tot = [], [], [], [], []
        for hh, p, d in groups:
            slot = 2 * hh + d
            cols_beta, cols_gcum, tot = [], [], []
            for i in range(pack):
                g_raw = gate_ref[0, d * H_A + h0 + hh, pl.ds(c0s[p] + i, 1), :]
                b_raw = gate_ref[0, 2 * H_A + d * H_A + h0 + hh, pl.ds(c0s[p] + i, 1), :]
                g_row = neg_a[slot] * _softplus(g_raw + dtb[slot])
                gcum = jnp.sum(jnp.where(incl[d], jnp.broadcast_to(g_row, (CHUNK, CHUNK)), 0.0), axis=1,
                               keepdims=True)
                cols_gcum.append(gcum)
                cols_beta.append(to_col(jax.nn.sigmoid(b_raw)))
                tot.append(gcum[last[d]:last[d] + 1, :])
            gcum_p = pack_cols(cols_gcum)
            gcum_row = jnp.sum(jnp.where(eye_p, gcum_p, 0.0), axis=0, keepdims=True)
            decay = jnp.exp(jnp.where(incl_p[d], gcum_p - gcum_row, -jnp.inf))
            a_mat.append(jnp.where(strict_p[d], kk[hh, p] * pack_cols(cols_beta) * decay, 0.0))
            qk.append((qk_raw[hh, p] * decay).astype(BF16))
            beta_c.append(cols_beta)
            gcum_c.append(cols_gcum)
            g_tot.append(tot)
        tm = [jnp.where(eye_p, 1.0, -jnp.where(pair_p[0], a, 0.0)) for a in a_mat]
        for lvl in range(1, len(pair_p)):
            tb = [t.astype(BF16) for t in tm]
            x = [_dot(jnp.where(pair_p[lvl], a, 0.0).astype(BF16), block_diag(t)) for a, t in zip(a_mat, tb)]
            tm = [t - _dot(t16, block_diag(xi.astype(BF16))) for t, t16, xi in zip(tm, tb, x)]
        nm = [jnp.where(eye_p, 0.0, t).astype(BF16) for t in tm]
        rows = [slice(i * CHUNK, (i + 1) * CHUNK) for i in range(pack)]
        eg_c = [[jnp.exp(x) for x in cols] for cols in gcum_c]
        rhs = [jnp.concatenate(
            [jnp.concatenate([vt[hh, p][rows[i]] * beta_c[g][i], kt[hh, p][rows[i]] * (beta_c[g][i] * eg_c[g][i])],
                             axis=-1) for i in range(pack)], axis=0)
            for g, (hh, p, d) in enumerate(groups)]
        rhsb = [r.astype(BF16) for r in rhs]
        zero_p = jnp.zeros((CHUNK, width), BF16)
        sol = [jnp.concatenate([(rhs[g][rows[i]] + _dot(jnp.where(in_blk[i], nm[g], zero_p), rhsb[g])).astype(BF16)
                                for i in range(pack)], axis=0) for g in range(len(groups))]
        local = [jnp.concatenate([_dot(jnp.where(in_blk[i], qk[g], zero_p), sol[g]) for i in range(pack)], axis=0)
                 for g in range(len(groups))]
        k_dec = [[(kt[hh, p][rows[i]] * jnp.exp(g_tot[g][i] - gcum_c[g][i])).astype(BF16) for i in range(pack)]
                 for g, (hh, p, d) in enumerate(groups)]
        nm_mat = [[_dot_tn(k_dec[g][i], sol[g][rows[i]]) for i in range(pack)]
                  for g in range(len(groups))]
        for g, (hh, p, d) in enumerate(groups):
            slot = 2 * hh + d
            obuf[slot, pl.ds(offs[p], width), :] = local[g][:, :DV_A]
            q_eff = [qt[hh, p][rows[i]] * eg_c[g][i] - local[g][rows[i], DV_A:] for i in range(pack)]
            qeff[slot, pl.ds(offs[p], width), :] = jnp.concatenate(q_eff, axis=0).astype(BF16)
            for i in range(pack):
                nmat[slot, c0s[p] + i] = nm_mat[g][i][:, :DV_A]
                mneg[slot, c0s[p] + i] = (-nm_mat[g][i][:, DV_A:]).astype(BF16)
                egt[slot, c0s[p] + i] = jnp.broadcast_to(jnp.exp(g_tot[g][i]), (8, DV_A))
        return carry

    lax.fori_loop(0, n_chunks // unroll, prepare, 0)

    for hh, d in slots:
        st[2 * hh + d] = (s0f_ref, s0b_ref)[d][0, hh]

    def scan(ci, carry):
        states = [st[slot] for slot in range(len(slots))]
        sb16 = [s.astype(BF16) for s in states]
        cs = [ci if d == 0 else n_chunks - 1 - ci for hh, d in slots]
        for slot, c in enumerate(cs):
            st[slot] = states[slot] * egt[slot, c][0:1, :] + _dot(mneg[slot, c], sb16[slot]) + nmat[slot, c]
        for slot, c in enumerate(cs):
            off = pl.multiple_of(c * CHUNK, CHUNK)
            obuf[slot, pl.ds(off, CHUNK), :] += _dot(qeff[slot, pl.ds(off, CHUNK), :], sb16[slot])
        return carry

    lax.fori_loop(0, n_chunks, scan, 0, unroll=4)
    for hh, d in slots:
        (sf_ref, sb_ref)[d][0, hh] = st[2 * hh + d]

    for hh in range(heads):
        cols = slice(hh * DV_A, (hh + 1) * DV_A)
        z =z_ref[:, cols].astype(F32)
        o = _rms(obuf[2 * hh] + obuf[2 * hh + 1], DV_A) * on_ref[...] * _silu(z)
        o_ref[:, cols] = o.astype(BF16)


def _gdn(proj, gates, conv_w, a_log, dt_bias, out_norm, s0f, s0b, batch, seq):
    n_chunks = seq // CHUNK
    unroll = math.gcd(n_chunks, GDN_UNROLL)
    pack = math.gcd(unroll, GDN_PACK if unroll > GDN_PACK else GDN_PACK // 2)
    hb = min(4, max(2, GDN_CHUNK_HEADS // n_chunks))
    groups, slots = H_A // hb, 2 * hb
    smem = pl.BlockSpec(memory_space=pltpu.SMEM)
    col = lambda off: pl.BlockSpec((seq, hb * DK_A), lambda b, h, off=off: (b, off + h))
    cw = lambda off: pl.BlockSpec((3, hb * DK_A), lambda b, h, off=off: (0, off + h))
    state = pl.BlockSpec((1, hb, DK_A, DV_A), lambda b, h: (b, h, 0, 0))
    st_shape = jax.ShapeDtypeStruct((batch, H_A, DK_A, DV_A), F32)
    return pl.pallas_call(
        functools.partial(_gdn_kernel, seq=seq, heads=hb, unroll=unroll, pack=pack),
        grid=(batch, groups),
        in_specs=[smem, smem, col(0), col(groups), col(2 * groups), col(3 * groups),
                  cw(0), cw(groups), cw(2 * groups),
                  pl.BlockSpec((1, 4 * H_A, n_chunks, CHUNK), lambda b, h: (b, 0, 0, 0)),
                  state, state, pl.BlockSpec((1, DV_A), lambda b, h: (0, 0))],
        out_specs=[pl.BlockSpec((seq, hb * DV_A), lambda b, h: (b, h)), state, state],
        out_shape=[jax.ShapeDtypeStruct((batch * seq, WV_A), BF16), st_shape, st_shape],
        scratch_shapes=[pltpu.VMEM((hb, seq, DK_A), F32), pltpu.VMEM((hb, seq, DK_A), F32),
                        pltpu.VMEM((hb, seq, DV_A), F32),
                        pltpu.VMEM((slots, seq, DV_A), F32), pltpu.VMEM((slots, seq, DK_A), BF16),
                        pltpu.VMEM((slots, n_chunks, DK_A, DV_A), F32),
                        pltpu.VMEM((slots, n_chunks, DK_A, DK_A), BF16),
                        pltpu.VMEM((slots, n_chunks, 8, DV_A), F32), pltpu.VMEM((slots, DK_A, DV_A), F32)],
        compiler_params=_cparams("parallel", "parallel"),
        name="gdn",
    )(a_log, dt_bias, proj, proj, proj, proj, conv_w, conv_w, conv_w, gates, s0f, s0b, out_norm)


def _pad_cols(w, n):
    return jnp.pad(w, ((0, 0), (0, n - w.shape[1])))


def kernel(x_prompt, x_sample, state_gdn_fwd, state_gdn_bwd, cache_mla_ckv, cache_mla_krope, cache_gqa_k, cache_gqa_v, c, c_ctx, norm_mix, norm_mlp, w_mod, b_mod, w_mlp_in, w_mlp_out, gdn_w_in, gdn_conv, gdn_a_log, gdn_dt_bias, gdn_out_norm, gdn_w_out, mla_w_down, mla_q_lat_norm, mla_kv_lat_norm, mla_w_uq, mla_w_ukv, mla_qn_nope, mla_qn_rope, mla_kn_nope, mla_kn_rope, mla_w_out, gqa_w_in, gqa_q_norm, gqa_k_norm, gqa_w_out):
    bp, sp, d = x_prompt.shape
    bs, ss, _ = x_sample.shape
    past = cache_mla_ckv.shape[2]
    xs = [x_prompt.reshape(bp * sp, d), x_sample.reshape(bs * ss, d)]
    batches, seqs = (bp, bs), (sp, ss)
    rows_per_batch = (None, ss)
    tms = (1024, 1024)

    cond = jnp.zeros((MOD_ROWS, d), F32).at[:bs].set(c).at[CTX_ROW].set(c_ctx)
    mods = _adaln(cond, w_mod, b_mod).reshape(DEPTH, MOD_ROWS, N_MOD, d)
    mods = jnp.pad(mods, ((0, 0), (0, 0), (0, MOD_PAD - N_MOD), (0, 0)))

    gqa_tables = _rope_tables(ss, HD_C, HD_C)
    mla_tables = _rope_tables(ss, ROPE_B, LANE)

    new_gdn_f, new_gdn_b, new_ckv, new_kr, new_k, new_v = [], [], [], [], [], []
    for i in range(DEPTH):
        kind, j = i % N_MIXERS, i // N_MIXERS
        mod = mods[i]
        g_mix = norm_mix[i].reshape(1, d)
        g_mlp = norm_mlp[i].reshape(1, d)
        mixed = []
        if kind == 0:
            w_in = gdn_w_in[j]
            n_main = 2 * WK_A + 2 * WV_A
            w_main = w_in[:, :n_main].astype(BF16)
            w_gate = _pad_cols(w_in[:, n_main:], LANE).astype(BF16)
            w_o = gdn_w_out[j].astype(BF16)
            on = gdn_out_norm[j].reshape(1, DV_A)
            for gi in range(2):
                b, s, tm = batches[gi], seqs[gi], tms[gi]
                proj, gb = _normproj(xs[gi], mod, g_mix, w_main, rows_per_batch[gi], tm, 1024, BF16, w_gate)
                gates = gb[:, :4 * H_A].reshape(b, s // CHUNK, CHUNK, 4 * H_A).transpose(0, 3, 1, 2)
                if gi == 0:
                    s0f = s0b = jnp.zeros((b, H_A, DK_A, DV_A), F32)
                else:
                    s0f, s0b = state_gdn_fwd[:, j], state_gdn_bwd[:, j]
                o, s_f, s_b = _gdn(proj, gates, gdn_conv[j], gdn_a_log[j], gdn_dt_bias[j], on, s0f, s0b, b, s)
                if gi == 0:
                    new_gdn_f.append(s_f)
                    new_gdn_b.append(s_b)
                mixed.append((o, w_o))
        elif kind == 1:
            w_uq = mla_w_uq[j].reshape(Q_LORA, H_B, NOPE_B + ROPE_B)
            w_uq = jnp.pad(w_uq, ((0, 0), (0, 0), (0, SLOT_B - NOPE_B - ROPE_B))).reshape(Q_LORA, H_B * SLOT_B)
            p = dict(q_lat_g=mla_q_lat_norm[j].reshape(1, Q_LORA), kv_lat_g=mla_kv_lat_norm[j].reshape(1, KV_LORA),
                     kn_rope=_pad_cols(mla_kn_rope[j].reshape(1, ROPE_B), LANE),
                     qn_rope=_pad_cols(mla_qn_rope[j].reshape(1, ROPE_B), LANE),
                     qn_nope=mla_qn_nope[j].reshape(1, NOPE_B), kn_nope=mla_kn_nope[j].reshape(1, NOPE_B),
                     w_uq=w_uq.astype(BF16), w_ukv=mla_w_ukv[j].astype(BF16))
            w_down = _pad_cols(mla_w_down[j], DOWN_PAD).astype(BF16)
            w_o = mla_w_out[j].astype(BF16)
            for gi in range(2):
                b, s, tm = batches[gi], seqs[gi], tms[gi]
                proj = _normproj(xs[gi], mod, g_mix, w_down, rows_per_batch[gi], tm, DOWN_PAD)
                q, k, v, ckv, kr = _mla_prep(proj, p, mla_tables if gi == 1 else None, s, 512)
                ctx = None
                if gi == 0:
                    new_ckv.append(ckv.reshape(b, s, KV_LORA))
                    new_kr.append(kr[:, :ROPE_B].reshape(b, s, ROPE_B))
                else:
                    ckv_c = cache_mla_ckv[:, j].reshape(b * past, KV_LORA)
                    kr_c = _pad_cols(cache_mla_krope[:, j].reshape(b * past, ROPE_B), LANE)
                    ctx = _mla_ctx(ckv_c, kr_c, p, 512)
                o = _attention(q, k, v, ctx, batch=b, seq=s, heads_kv=H_B, kv_per_step=4, group=1, dk=SLOT_B,
                               dv=V_B, tq=256)
                mixed.append((o, w_o))
        else:
            w_in = gqa_w_in[j].astype(BF16)
            w_o = gqa_w_out[j].astype(BF16)
            q_g, k_g = gqa_q_norm[j].reshape(1, HD_C), gqa_k_norm[j].reshape(1, HD_C)
            for gi in range(2):
                b, s, tm = batches[gi], seqs[gi], tms[gi]
                q, k, v, kf, vf = _gqa_prep(xs[gi], mod, g_mix, w_in, q_g, k_g, gqa_tables if gi == 1 else None,
                                            rows_per_batch[gi], s, 512)
                ctx = None
                if gi == 0:
                    new_k.append(kf.reshape(b, s, KVH_C, HD_C))
                    new_v.append(vf.reshape(b, s, KVH_C, HD_C))
                else:
                    ctx = (cache_gqa_k[:, j].reshape(b * past, KVH_C * HD_C).astype(BF16),
                           cache_gqa_v[:, j].reshape(b * past, KVH_C * HD_C).astype(BF16))
                o = _attention(q, k, v, ctx, batch=b, seq=s, heads_kv=KVH_C, kv_per_step=2, group=H_C // KVH_C,
                               dk=HD_C, dv=HD_C, tq=256)
                mixed.append((o, w_o))
        w_mi, w_mo = w_mlp_in[i].astype(BF16), w_mlp_out[i].astype(BF16)
        for gi in range(2):
            o, w_o = mixed[gi]
            xs[gi] = _mix_mlp(xs[gi], o, mod, g_mlp, w_o, w_mi, w_mo, rows_per_batch[gi], tms[gi], 1024)

    dt = x_prompt.dtype
    stack = lambda lst: jnp.stack(lst, axis=1).astype(dt)
    return (xs[0].reshape(bp, sp, d), xs[1].reshape(bs, ss, d), stack(new_gdn_f), stack(new_gdn_b),
            stack(new_ckv), stack(new_kr), stack(new_k), stack(new_v))
```

```python
import functools
import math

import jax
import jax.numpy as jnp
from jax import lax
from jax.experimental import pallas as pl
from jax.experimental.pallas import tpu as pltpu

F32 = jnp.float32
BF16 = jnp.bfloat16

D_MODEL = 1024
DEPTH = 4
GRID_W = 64
N_MIXERS = 3
D_FF = 4 * D_MODEL
N_MOD = 6
EPS = 1e-6
ROPE_THETA = 10000.0
H_A = 8
DK_A = 128
DV_A = 128
WK_A = H_A * DK_A
WV_A = H_A * DV_A
CHUNK = 64
H_B = 8
Q_LORA = 384
KV_LORA = 256
NOPE_B = 128
ROPE_B = 64
V_B = 128
SLOT_B = 256
DOWN_PAD = 768
H_C = 8
KVH_C = 2
HD_C = 128

LOG2_E = math.log2(math.e)
LANE = 128
MOD_ROWS = 16
CTX_ROW = 8
MOD_PAD = 8
VMEM_LIMIT = 56 * 1024 * 1024
TM_TOKENS = 1024
TM_PREP = 512
TN_PROJ = 1024
TF_MLP = 1024
TN_ADALN = 1024
TQ_ATTN = 256


def _cparams(*sem):
    return pltpu.CompilerParams(dimension_semantics=sem, vmem_limit_bytes=VMEM_LIMIT)


def _rms(x, n):
    return x * lax.rsqrt(jnp.sum(x * x, axis=-1, keepdims=True) * (1.0 / n) + EPS)


def _dot(a, b):
    return jnp.dot(a, b, preferred_element_type=F32)


def _dot_nt(a, b):
    return lax.dot_general(a, b, (((1,), (1,)), ((), ())), preferred_element_type=F32)


def _dot_tn(a, b):
    return lax.dot_general(a, b, (((0,), (0,)), ((), ())), preferred_element_type=F32)


def _mod_index(rows_per_batch, tm):
    if rows_per_batch is None:
        return lambda t, *_: (CTX_ROW, 0, 0)
    return lambda t, *_: ((t * tm) // rows_per_batch, 0, 0)


def _adaln_kernel(c_ref, w_ref, b_ref, o_ref):
    c = c_ref[...]
    h = (c * jax.nn.sigmoid(c)).astype(BF16)
    o_ref[0] = _dot(h, w_ref[0].astype(BF16)) + b_ref[0]


def _adaln(cond, w_mod, b_mod):
    tn = TN_ADALN
    n = N_MOD * D_MODEL
    return pl.pallas_call(
        _adaln_kernel,
        grid=(DEPTH, n // tn),
        in_specs=[pl.BlockSpec((MOD_ROWS, D_MODEL), lambda i, j: (0, 0)),
                  pl.BlockSpec((1, D_MODEL, tn), lambda i, j: (i, 0, j)),
                  pl.BlockSpec((1, 1, tn), lambda i, j: (i, 0, j))],
        out_specs=pl.BlockSpec((1, MOD_ROWS, tn), lambda i, j: (i, 0, j)),
        out_shape=jax.ShapeDtypeStruct((DEPTH, MOD_ROWS, n), F32),
        compiler_params=_cparams("parallel", "parallel"),
        name="adaln",
    )(cond, w_mod, b_mod.reshape(DEPTH, 1, n))


def _modnorm(x, g, mod, shift_row, scale_row):
    y = _rms(x, D_MODEL) * g
    return y * (1.0 + mod[scale_row:scale_row + 1, :]) + mod[shift_row:shift_row + 1, :]


def _normproj_kernel(*refs, has_aux):
    if has_aux:
        x_ref, mod_ref, g_ref, w_ref, wa_ref, o_ref, oa_ref, h_ref = refs
    else:
        x_ref, mod_ref, g_ref, w_ref, o_ref, h_ref = refs

    j = pl.program_id(1)
    tn = o_ref.shape[1]

    @pl.when(j == 0)
    def _():
        h_ref[...] = _modnorm(x_ref[...], g_ref[...], mod_ref[0], 0, 1).astype(BF16)
        if has_aux:
            oa_ref[...] = _dot(h_ref[...], wa_ref[...])

    w = w_ref[:, pl.ds(pl.multiple_of(j * tn, tn), tn)]
    o_ref[...] = _dot(h_ref[...], w).astype(o_ref.dtype)


def _normproj(x, mod, g, w, rows_per_batch, tm, tn, out_dtype=F32, w_aux=None):
    t, n = x.shape[0], w.shape[1]
    in_specs = [pl.BlockSpec((tm, D_MODEL), lambda i, j: (i, 0)),
                pl.BlockSpec((1, MOD_PAD, D_MODEL), _mod_index(rows_per_batch, tm)),
                pl.BlockSpec((1, D_MODEL), lambda i, j: (0, 0)),
                pl.BlockSpec((D_MODEL, n), lambda i, j: (0, 0))]
    out_specs = [pl.BlockSpec((tm, tn), lambda i, j: (i, j))]
    out_shape = [jax.ShapeDtypeStruct((t, n), out_dtype)]
    args = [x, mod, g, w]
    if w_aux is not None:
        in_specs.append(pl.BlockSpec((D_MODEL, LANE), lambda i, j: (0, 0)))
        out_specs.append(pl.BlockSpec((tm, LANE), lambda i, j: (i, 0)))
        out_shape.append(jax.ShapeDtypeStruct((t, LANE), F32))
        args.append(w_aux)
    out = pl.pallas_call(
        functools.partial(_normproj_kernel, has_aux=w_aux is not None),
        grid=(t // tm, n // tn),
        in_specs=in_specs,
        out_specs=out_specs,
        out_shape=out_shape,
        scratch_shapes=[pltpu.VMEM((tm, D_MODEL), BF16)],
        compiler_params=_cparams("parallel", "arbitrary"),
        name="normproj",
    )(*args)
    return out if w_aux is not None else out[0]


def _mix_mlp_kernel(x_ref, a_ref, mod_ref, g_ref, wo_ref, win_ref, wout_ref, o_ref, h_ref, acc_ref):
    f = pl.program_id(1)

    @pl.when(f == 0)
    def _():
        o_ref[...] = x_ref[...] + mod_ref[0, 2:3, :] * _dot(a_ref[...], wo_ref[...])
        h_ref[...] = _modnorm(o_ref[...], g_ref[...], mod_ref[0], 3, 4).astype(BF16)
        acc_ref[...] = jnp.zeros_like(acc_ref)

    a = jnp.maximum(_dot(h_ref[...], win_ref[...]), 0.0)
    acc_ref[...] += _dot((a * a).astype(BF16), wout_ref[...])

    @pl.when(f == pl.num_programs(1) - 1)
    def _():
        o_ref[...] += mod_ref[0, 5:6, :] * acc_ref[...]


def _mix_mlp(x, a, mod, g, w_o, w_in, w_out, layer, rows_per_batch, tm, tf):
    t = x.shape[0]
    return pl.pallas_call(
        _mix_mlp_kernel,
        grid=(t // tm, D_FF // tf),
        in_specs=[pl.BlockSpec((tm, D_MODEL), lambda i, f: (i, 0)),
                  pl.BlockSpec((tm, D_MODEL), lambda i, f: (i, 0)),
                  pl.BlockSpec((1, MOD_PAD, D_MODEL), _mod_index(rows_per_batch, tm)),
                  pl.BlockSpec((1, D_MODEL), lambda i, f: (0, 0)),
                  pl.BlockSpec((D_MODEL, D_MODEL), lambda i, f: (0, 0)),
                  pl.BlockSpec((None, D_MODEL, tf), lambda i, f: (layer, 0, f)),
                  pl.BlockSpec((None, tf, D_MODEL), lambda i, f: (layer, f, 0))],
        out_specs=pl.BlockSpec((tm, D_MODEL), lambda i, f: (i, 0)),
        out_shape=jax.ShapeDtypeStruct((t, D_MODEL), F32),
        scratch_shapes=[pltpu.VMEM((tm, D_MODEL), BF16), pltpu.VMEM((tm, D_MODEL), F32)],
        compiler_params=_cparams("parallel", "arbitrary"),
        name="mix_mlp",
    )(x, a, mod, g, w_o, w_in, w_out)


def _attn_kernel(*refs, kv_per_step, group, dk, dv, has_ctx):
    if has_ctx:
        q_ref, k_ref, v_ref, kc_ref, vc_ref, o_ref = refs
    else:
        q_ref, k_ref, v_ref, o_ref = refs

    def scores(n):
        j = n // group
        q = q_ref[:, n * dk:(n + 1) * dk]
        s = _dot_nt(q, k_ref[:, j * dk:(j + 1) * dk])
        sc = _dot_nt(q, kc_ref[:, j * dk:(j + 1) * dk]) if has_ctx else None
        return s, sc

    heads = kv_per_step * group
    nxt = scores(0)
    for n in range(heads):
        s, sc = nxt
        if n + 1 < heads:
            nxt = scores(n + 1)
        j = n // group
        m = jnp.max(s, axis=-1, keepdims=True)
        if has_ctx:
            m = jnp.maximum(m, jnp.max(sc, axis=-1, keepdims=True))
            pc = jnp.exp2(sc - m)
        p = jnp.exp2(s - m)
        l = jnp.sum(p, axis=-1, keepdims=True)
        o = _dot(p.astype(BF16), v_ref[:, j * dv:(j + 1) * dv])
        if has_ctx:
            l = l + jnp.sum(pc, axis=-1, keepdims=True)
            o = o + _dot(pc.astype(BF16), vc_ref[:, j * dv:(j + 1) * dv])
        o_ref[:, n * dv:(n + 1) * dv] = (o / l).astype(o_ref.dtype)


def _attention(q, k, v, ctx, *, batch, seq, heads_kv, kv_per_step, group, dk, dv, tq):
    nq = seq // tq
    hp = kv_per_step
    in_specs = [pl.BlockSpec((tq, hp * group * dk), lambda b, j, i: (b * nq + i, j)),
                pl.BlockSpec((seq, hp * dk), lambda b, j, i: (b, j)),
                pl.BlockSpec((seq, hp * dv), lambda b, j, i: (b, j))]
    args = [q, k, v]
    if ctx is not None:
        past = ctx[0].shape[0] // batch
        in_specs += [pl.BlockSpec((past, hp * dk), lambda b, j, i: (b, j)),
                     pl.BlockSpec((past, hp * dv), lambda b, j, i: (b, j))]
        args += list(ctx)
    return pl.pallas_call(
        functools.partial(_attn_kernel, kv_per_step=hp, group=group, dk=dk, dv=dv, has_ctx=ctx is not None),
        grid=(batch, heads_kv // hp, nq),
        in_specs=in_specs,
        out_specs=pl.BlockSpec((tq, hp * group * dv), lambda b, j, i: (b * nq + i, j)),
        out_shape=jax.ShapeDtypeStruct((batch * seq, heads_kv * group * dv), BF16),
        compiler_params=_cparams("parallel", "parallel", "parallel"),
        name="attention",
    )(*args)


def _swap_quarters(x, quarter):
    lanes = x.shape[-1]
    lane = lax.broadcasted_iota(jnp.int32, x.shape, x.ndim - 1)
    up = pltpu.roll(x, lanes - quarter, x.ndim - 1)
    down = pltpu.roll(x, quarter, x.ndim - 1)
    return jnp.where((lane // quarter) % 2 == 0, up, down)


def _rope(x, cos, sin, quarter):
    return x * cos + _swap_quarters(x, quarter) * sin


def _rope_tables(seq, width, pad_to):
    rows = seq // GRID_W
    row = jnp.repeat(jnp.arange(rows, dtype=F32), GRID_W)
    col = jnp.tile(jnp.arange(GRID_W, dtype=F32), rows)
    quarter = width // 4
    freqs = ROPE_THETA ** (-jnp.arange(quarter, dtype=F32) / quarter)
    ar = row[:, None] * freqs[None, :]
    ac = col[:, None] * freqs[None, :]
    cos = jnp.concatenate([jnp.cos(ar), jnp.cos(ar), jnp.cos(ac), jnp.cos(ac)], axis=-1)
    sin = jnp.concatenate([-jnp.sin(ar), jnp.sin(ar), -jnp.sin(ac), jnp.sin(ac)], axis=-1)
    pad = ((0, 0), (0, pad_to - width))
    return jnp.pad(cos, pad), jnp.pad(sin, pad)


def _gqa_prep_kernel(*refs, use_rope):
    if use_rope:
        x_ref, mod_ref, g_ref, w_ref, qg_ref, kg_ref, cos_ref, sin_ref, q_ref, k_ref, v_ref, kf_ref, vf_ref = refs
    else:
        x_ref, mod_ref, g_ref, w_ref, qg_ref, kg_ref, q_ref, k_ref, v_ref, kf_ref, vf_ref = refs
    proj = _dot(_modnorm(x_ref[...], g_ref[...], mod_ref[0], 0, 1).astype(BF16), w_ref[...])
    for h in range(H_C + KVH_C):
        x = _rms(proj[:, h * HD_C:(h + 1) * HD_C], HD_C)
        x = x * (qg_ref[...] if h < H_C else kg_ref[...])
        if h >= H_C:
            kf_ref[:, (h - H_C) * HD_C:(h - H_C + 1) * HD_C] = x
        if use_rope:
            x = _rope(x, cos_ref[...], sin_ref[...], HD_C // 4)
        if h < H_C:
            q_ref[:, h * HD_C:(h + 1) * HD_C] = (x * (LOG2_E * HD_C ** -0.5)).astype(BF16)
        else:
            k_ref[:, (h - H_C) * HD_C:(h - H_C + 1) * HD_C] = x.astype(BF16)
    vv = proj[:, (H_C + KVH_C) * HD_C:]
    vf_ref[...] = vv
    v_ref[...] = vv.astype(BF16)


def _gqa_prep(x, mod, g, w, q_g, k_g, tables, rows_per_batch, seq, tm):
    t = x.shape[0]
    nq, nkv = H_C * HD_C, KVH_C * HD_C
    in_specs = [pl.BlockSpec((tm, D_MODEL), lambda i: (i, 0)),
                pl.BlockSpec((1, MOD_PAD, D_MODEL), _mod_index(rows_per_batch, tm)),
                pl.BlockSpec((1, D_MODEL), lambda i: (0, 0)),
                pl.BlockSpec((D_MODEL, nq + 2 * nkv), lambda i: (0, 0)),
                pl.BlockSpec((1, HD_C), lambda i: (0, 0)),
                pl.BlockSpec((1, HD_C), lambda i: (0, 0))]
    args = [x, mod, g, w, q_g, k_g]
    if tables is not None:
        per = seq // tm
        in_specs += [pl.BlockSpec((tm, HD_C), lambda i: (i % per, 0))] * 2
        args += list(tables)
    return pl.pallas_call(
        functools.partial(_gqa_prep_kernel, use_rope=tables is not None),
        grid=(t // tm,),
        in_specs=in_specs,
        out_specs=[pl.BlockSpec((tm, nq), lambda i: (i, 0))] + [pl.BlockSpec((tm, nkv), lambda i: (i, 0))] * 4,
        out_shape=[jax.ShapeDtypeStruct((t, nq), BF16), jax.ShapeDtypeStruct((t, nkv), BF16),
                   jax.ShapeDtypeStruct((t, nkv), BF16), jax.ShapeDtypeStruct((t, nkv), F32),
                   jax.ShapeDtypeStruct((t, nkv), F32)],
        compiler_params=_cparams("parallel"),
        name="gqa_prep",
    )(*args)


def _mla_keys_values(ckv, krope_slot, wukv_ref, knn_ref, k_ref, v_ref):
    kv = _dot(ckv.astype(BF16), wukv_ref[...])
    kr = krope_slot.astype(BF16)
    for h in range(H_B):
        base = h * (NOPE_B + V_B)
        kn = _rms(kv[:, base:base + NOPE_B], NOPE_B) * knn_ref[...]
        k_ref[:, h * SLOT_B:h * SLOT_B + NOPE_B] = kn.astype(BF16)
        k_ref[:, h * SLOT_B + NOPE_B:(h + 1) * SLOT_B] = kr
        v_ref[:, h * V_B:(h + 1) * V_B] = kv[:, base + NOPE_B:base + NOPE_B + V_B].astype(BF16)


def _mla_prep_kernel(*refs, use_rope):
    if use_rope:
        (p_ref, qlg_ref, kvg_ref, krg_ref, wuq_ref, wukv_ref, qnn_ref, qnr_ref, knn_ref, cos_ref, sin_ref,
         q_ref, k_ref, v_ref, ckv_ref, kr_ref) = refs
    else:
        (p_ref, qlg_ref, kvg_ref, krg_ref, wuq_ref, wukv_ref, qnn_ref, qnr_ref, knn_ref,
         q_ref, k_ref, v_ref, ckv_ref, kr_ref) = refs
    cq = _rms(p_ref[:, :Q_LORA], Q_LORA) * qlg_ref[...]
    ckv = _rms(p_ref[:, Q_LORA:Q_LORA + KV_LORA], KV_LORA) * kvg_ref[...]
    kr = _rms(p_ref[:, Q_LORA + KV_LORA:], ROPE_B) * krg_ref[...]
    ckv_ref[...] = ckv
    kr_ref[...] = kr
    q = _dot(cq.astype(BF16), wuq_ref[...])
    scale = LOG2_E * (NOPE_B + ROPE_B) ** -0.5
    for h in range(H_B):
        qn = _rms(q[:, h * SLOT_B:h * SLOT_B + NOPE_B], NOPE_B) * qnn_ref[...]
        qr = _rms(q[:, h * SLOT_B + NOPE_B:(h + 1) * SLOT_B], ROPE_B) * qnr_ref[...]
        if use_rope:
            qr = _rope(qr, cos_ref[...], sin_ref[...], ROPE_B // 4)
        q_ref[:, h * SLOT_B:h * SLOT_B + NOPE_B] = (qn * scale).astype(BF16)
        q_ref[:, h * SLOT_B + NOPE_B:(h + 1) * SLOT_B] = (qr * scale).astype(BF16)
    if use_rope:
        kr = _rope(kr, cos_ref[...], sin_ref[...], ROPE_B // 4)
    _mla_keys_values(ckv, kr, wukv_ref, knn_ref, k_ref, v_ref)


def _mla_prep(proj, p, tables, seq, tm):
    t = proj.shape[0]
    full = lambda shape: pl.BlockSpec(shape, lambda i: (0, 0))
    in_specs = [pl.BlockSpec((tm, DOWN_PAD), lambda i: (i, 0)),
                full((1, Q_LORA)), full((1, KV_LORA)), full((1, LANE)),
                full((Q_LORA, H_B * SLOT_B)), full((KV_LORA, H_B * (NOPE_B + V_B))),
                full((1, NOPE_B)), full((1, LANE)), full((1, NOPE_B))]
    args = [proj, p["q_lat_g"], p["kv_lat_g"], p["kn_rope"], p["w_uq"], p["w_ukv"], p["qn_nope"], p["qn_rope"],
            p["kn_nope"]]
    if tables is not None:
        per = seq // tm
        in_specs += [pl.BlockSpec((tm, LANE), lambda i: (i % per, 0))] * 2
        args += list(tables)
    row = lambda n: pl.BlockSpec((tm, n), lambda i: (i, 0))
    return pl.pallas_call(
        functools.partial(_mla_prep_kernel, use_rope=tables is not None),
        grid=(t // tm,),
        in_specs=in_specs,
        out_specs=[row(H_B * SLOT_B), row(H_B * SLOT_B), row(H_B * V_B), row(KV_LORA), row(LANE)],
        out_shape=[jax.ShapeDtypeStruct((t, H_B * SLOT_B), BF16), jax.ShapeDtypeStruct((t, H_B * SLOT_B), BF16),
                   jax.ShapeDtypeStruct((t, H_B * V_B), BF16), jax.ShapeDtypeStruct((t, KV_LORA), F32),
                   jax.ShapeDtypeStruct((t, LANE), F32)],
        compiler_params=_cparams("parallel"),
        name="mla_prep",
    )(*args)


def _mla_ctx_kernel(ckv_ref, kr_ref, wukv_ref, knn_ref, k_ref, v_ref):
    _mla_keys_values(ckv_ref[...], kr_ref[...], wukv_ref, knn_ref, k_ref, v_ref)


def _mla_ctx(ckv, krope_slot, p, tm):
    t = ckv.shape[0]
    full = lambda shape: pl.BlockSpec(shape, lambda i: (0, 0))
    row = lambda n: pl.BlockSpec((tm, n), lambda i: (i, 0))
    return pl.pallas_call(
        _mla_ctx_kernel,
        grid=(t // tm,),
        in_specs=[row(KV_LORA), row(LANE), full((KV_LORA, H_B * (NOPE_B + V_B))), full((1, NOPE_B))],
        out_specs=[row(H_B * SLOT_B), row(H_B * V_B)],
        out_shape=[jax.ShapeDtypeStruct((t, H_B * SLOT_B), BF16), jax.ShapeDtypeStruct((t, H_B * V_B), BF16)],
        compiler_params=_cparams("parallel"),
        name="mla_ctx",
    )(ckv, krope_slot, p["w_ukv"], p["kn_nope"])


def _silu(y):
    h = 0.5 * y
    return h * jnp.tanh(h) + h


def _conv_silu(x, w):
    s = x.shape[0]
    row = lax.broadcasted_iota(jnp.int32, x.shape, 0)
    prev = jnp.where(row == 0, 0.0, pltpu.roll(x, 1, 0))
    nxt = jnp.where(row == s - 1, 0.0, pltpu.roll(x, s - 1, 0))
    y = prev * w[0:1, :] + x * w[1:2, :] + nxt * w[2:3, :]
    return _silu(y)


def _l2n(x):
    return x * lax.rsqrt(jnp.sum(x * x, axis=-1, keepdims=True) + EPS)


def _softplus(x):
    return jnp.maximum(x, 0.0) + jnp.log1p(jnp.exp(-jnp.abs(x)))


GDN_CHUNK_HEADS = 64
GDN_PACK = 4
GDN_UNROLL = 8


def _gdn_kernel(alog_ref, dtb_ref, q_ref, k_ref, v_ref, z_ref, cq_ref, ck_ref, cv_ref, gate_ref,
                s0f_ref, s0b_ref, on_ref, o_ref, sf_ref, sb_ref,
                qs, ks, vs, obuf, qeff, nmat, mneg, egt, st, *, seq, heads, unroll, pack):
    h0 = pl.program_id(1) * heads
    n_chunks = seq // CHUNK
    for hh in range(heads):
        cols = slice(hh * DK_A, (hh + 1) * DK_A)
        qs[hh] = _l2n(_conv_silu(q_ref[:, cols].astype(F32), cq_ref[:, cols])) * (DK_A ** -0.5)
        ks[hh] = _l2n(_conv_silu(k_ref[:, cols].astype(F32), ck_ref[:, cols]))
        vs[hh] = _conv_silu(v_ref[:, cols].astype(F32), cv_ref[:, cols])

    width = pack * CHUNK
    ii = lax.broadcasted_iota(jnp.int32, (CHUNK, CHUNK), 0)
    jj = lax.broadcasted_iota(jnp.int32, (CHUNK, CHUNK), 1)
    eye = ii == jj
    ip = lax.broadcasted_iota(jnp.int32, (CHUNK, width), 0)
    lane = lax.broadcasted_iota(jnp.int32, (CHUNK, width), 1)
    jp = lane % CHUNK
    in_blk = [(lane // CHUNK) == i for i in range(pack)]
    eye_p = ip == jp
    incl = (ii >= jj, ii <= jj)
    incl_p = (ip >= jp, ip <= jp)
    strict_p = (ip > jp, ip < jp)
    last = (CHUNK - 1, 0)
    pair_p = [(ip // 2) == (jp // 2)]
    pair_p += [((ip // (2 * s)) == (jp // (2 * s))) & ((ip // s) != (jp // s)) for s in (2, 4, 8, 16, 32)]
    slots = [(hh, d) for hh in range(heads) for d in range(2)]

    def pack_cols(cols):
        out = jnp.broadcast_to(cols[0], (CHUNK, width))
        for i in range(1, pack):
            out = jnp.where(in_blk[i], cols[i], out)
        return out

    def pack_diag(g):
        out = g[:CHUNK, :]
        for i in range(1, pack):
            out = jnp.where(in_blk[i], g[i * CHUNK:(i + 1) * CHUNK, :], out)
        return out

    def block_diag(x):
        return jnp.concatenate([jnp.where(in_blk[i], x, jnp.zeros_like(x)) for i in range(pack)], axis=0)
    neg_a = [-jnp.exp(jnp.full((1, CHUNK), alog_ref[d, h0 + hh], F32)) for hh, d in slots]
    dtb = [dtb_ref[d, h0 + hh] for hh, d in slots]

    def to_col(r):
        return jnp.sum(jnp.where(eye, jnp.broadcast_to(r, (CHUNK, CHUNK)), 0.0), axis=1, keepdims=True)

    def prepare(blk, carry):
        c0s = [blk * unroll + p * pack for p in range(unroll // pack)]
        offs = [pl.multiple_of(c0 * CHUNK, width) for c0 in c0s]
        tiles = [(hh, p) for hh in range(heads) for p in range(len(c0s))]
        qt = {t: qs[t[0], pl.ds(offs[t[1]], width), :] for t in tiles}
        kt = {t: ks[t[0], pl.ds(offs[t[1]], width), :] for t in tiles}
        vt = {t: vs[t[0], pl.ds(offs[t[1]], width), :] for t in tiles}
        ktb = {t: kt[t].astype(BF16) for t in tiles}
        kk = {t: pack_diag(_dot_nt(ktb[t], ktb[t])) for t in tiles}
        qk_raw = {t: pack_diag(_dot_nt(qt[t].astype(BF16), ktb[t])) for t in tiles}
        groups = [(hh, p, d) for hh, p in tiles for d in range(2)]
        a_mat, qk, beta_c, gcum_c, g_tot = [], [], [], [], []
        for hh, p, d in groups:
            slot = 2 * hh + d
            cols_beta, cols_gcum, tot = [], [], []
            for i in range(pack):
                g_raw = gate_ref[0, d * H_A + h0 + hh, pl.ds(c0s[p] + i, 1), :]
                b_raw = gate_ref[0, 2 * H_A + d * H_A + h0 + hh, pl.ds(c0s[p] + i, 1), :]
                g_row = neg_a[slot] * _softplus(g_raw + dtb[slot])
                gcum = jnp.sum(jnp.where(incl[d], jnp.broadcast_to(g_row, (CHUNK, CHUNK)), 0.0), axis=1,
                               keepdims=True)
                cols_gcum.append(gcum)
                cols_beta.append(to_col(jax.nn.sigmoid(b_raw)))
                tot.append(gcum[last[d]:last[d] + 1, :])
            gcum_p = pack_cols(cols_gcum)
            gcum_row = jnp.sum(jnp.where(eye_p, gcum_p, 0.0), axis=0, keepdims=True)
            decay = jnp.exp(jnp.where(incl_p[d], gcum_p - gcum_row, -jnp.inf))
            a_mat.append(jnp.where(strict_p[d], kk[hh, p] * pack_cols(cols_beta) * decay, 0.0))
            qk.append((qk_raw[hh, p] * decay).astype(BF16))
            beta_c.append(cols_beta)
            gcum_c.append(cols_gcum)
            g_tot.append(tot)
        tm = [jnp.where(eye_p, 1.0, -jnp.where(pair_p[0], a, 0.0)) for a in a_mat]
        for lvl in range(1, len(pair_p)):
            tb = [t.astype(BF16) for t in tm]
            x = [_dot(jnp.where(pair_p[lvl], a, 0.0).astype(BF16), block_diag(t)) for a, t in zip(a_mat, tb)]
            tm = [t - _dot(t16, block_diag(xi.astype(BF16))) for t, t16, xi in zip(tm, tb, x)]
        nm = [jnp.where(eye_p, 0.0, t).astype(BF16) for t in tm]
        rows = [slice(i * CHUNK, (i + 1) * CHUNK) for i in range(pack)]
        eg_c = [[jnp.exp(x) for x in cols] for cols in gcum_c]
        rhs = [jnp.concatenate(
            [jnp.concatenate([vt[hh, p][rows[i]] * beta_c[g][i], kt[hh, p][rows[i]] * (beta_c[g][i] * eg_c[g][i])],
                             axis=-1) for i in range(pack)], axis=0)
            for g, (hh, p, d) in enumerate(groups)]
        rhsb = [r.astype(BF16) for r in rhs]
        zero_p = jnp.zeros((CHUNK, width), BF16)
        sol = [jnp.concatenate([(rhs[g][rows[i]] + _dot(jnp.where(in_blk[i], nm[g], zero_p), rhsb[g])).astype(BF16)
                                for i in range(pack)], axis=0) for g in range(len(groups))]
        local = [jnp.concatenate([_dot(jnp.where(in_blk[i], qk[g], zero_p), sol[g]) for i in range(pack)], axis=0)
                 for g in range(len(groups))]
        k_dec = [[(kt[hh, p][rows[i]] * jnp.exp(g_tot[g][i] - gcum_c[g][i])).astype(BF16) for i in range(pack)]
                 for g, (hh, p, d) in enumerate(groups)]
        nm_mat = [[_dot_tn(k_dec[g][i], sol[g][rows[i]]) for i in range(pack)]
                  for g in range(len(groups))]
        for g, (hh, p, d) in enumerate(groups):
            slot = 2 * hh + d
            obuf[slot, pl.ds(offs[p], width), :] = local[g][:, :DV_A]
            q_eff = [qt[hh, p][rows[i]] * eg_c[g][i] - local[g][rows[i], DV_A:] for i in range(pack)]
            qeff[slot, pl.ds(offs[p], width), :] = jnp.concatenate(q_eff, axis=0).astype(BF16)
            for i in range(pack):
                nmat[slot, c0s[p] + i] = nm_mat[g][i][:, :DV_A]
                mneg[slot, c0s[p] + i] = (-nm_mat[g][i][:, DV_A:]).astype(BF16)
                egt[slot, c0s[p] + i] = jnp.broadcast_to(jnp.exp(g_tot[g][i]), (8, DV_A))
        return carry

    lax.fori_loop(0, n_chunks // unroll, prepare, 0)

    for hh, d in slots:
        st[2 * hh + d] = (s0f_ref, s0b_ref)[d][0, hh]

    def scan(ci, carry):
        states = [st[slot] for slot in range(len(slots))]
        sb16 = [s.astype(BF16) for s in states]
        cs = [ci if d == 0 else n_chunks - 1 - ci for hh, d in slots]
        for slot, c in enumerate(cs):
            st[slot] = states[slot] * egt[slot, c][0:1, :] + _dot(mneg[slot, c], sb16[slot]) + nmat[slot, c]
        for slot, c in enumerate(cs):
            off = pl.multiple_of(c * CHUNK, CHUNK)
            obuf[slot, pl.ds(off, CHUNK), :] += _dot(qeff[slot, pl.ds(off, CHUNK), :], sb16[slot])
        return carry

    lax.fori_loop(0, n_chunks, scan, 0, unroll=8 if n_chunks % 8 == 0 else 4)
    for hh, d in slots:
        (sf_ref, sb_ref)[d][0, hh] = st[2 * hh + d]

    for hh in range(heads):
        cols = slice(hh * DV_A, (hh + 1) * DV_A)
        z =z_ref[:, cols].astype(F32)
        o = _rms(obuf[2 * hh] + obuf[2 * hh + 1], DV_A) * on_ref[...] * _silu(z)
        o_ref[:, cols] = o.astype(BF16)


def _gdn(proj, gates, conv_w, a_log, dt_bias, out_norm, s0f, s0b, batch, seq):
    n_chunks = seq // CHUNK
    unroll = math.gcd(n_chunks, GDN_UNROLL)
    pack = math.gcd(unroll, GDN_PACK if unroll > GDN_PACK else GDN_PACK // 2)
    hb = min(4, max(2, GDN_CHUNK_HEADS // n_chunks))
    groups, slots = H_A // hb, 2 * hb
    smem = pl.BlockSpec(memory_space=pltpu.SMEM)
    col = lambda off: pl.BlockSpec((seq, hb * DK_A), lambda b, h, off=off: (b, off + h))
    cw = lambda off: pl.BlockSpec((3, hb * DK_A), lambda b, h, off=off: (0, off + h))
    state = pl.BlockSpec((1, hb, DK_A, DV_A), lambda b, h: (b, h, 0, 0))
    st_shape = jax.ShapeDtypeStruct((batch, H_A, DK_A, DV_A), F32)
    return pl.pallas_call(
        functools.partial(_gdn_kernel, seq=seq, heads=hb, unroll=unroll, pack=pack),
        grid=(batch, groups),
        in_specs=[smem, smem, col(0), col(groups), col(2 * groups), col(3 * groups),
                  cw(0), cw(groups), cw(2 * groups),
                  pl.BlockSpec((1, 4 * H_A, n_chunks, CHUNK), lambda b, h: (b, 0, 0, 0)),
                  state, state, pl.BlockSpec((1, DV_A), lambda b, h: (0, 0))],
        out_specs=[pl.BlockSpec((seq, hb * DV_A), lambda b, h: (b, h)), state, state],
        out_shape=[jax.ShapeDtypeStruct((batch * seq, WV_A), BF16), st_shape, st_shape],
        scratch_shapes=[pltpu.VMEM((hb, seq, DK_A), F32), pltpu.VMEM((hb, seq, DK_A), F32),
                        pltpu.VMEM((hb, seq, DV_A), F32),
                        pltpu.VMEM((slots, seq, DV_A), F32), pltpu.VMEM((slots, seq, DK_A), BF16),
                        pltpu.VMEM((slots, n_chunks, DK_A, DV_A), F32),
                        pltpu.VMEM((slots, n_chunks, DK_A, DK_A), BF16),
                        pltpu.VMEM((slots, n_chunks, 8, DV_A), F32), pltpu.VMEM((slots, DK_A, DV_A), F32)],
        compiler_params=_cparams("parallel", "parallel"),
        name="gdn",
    )(a_log, dt_bias, proj, proj, proj, proj, conv_w, conv_w, conv_w, gates, s0f, s0b, out_norm)


def _pad_cols(w, n):
    return jnp.pad(w, ((0, 0), (0, n - w.shape[1])))


def kernel(x_prompt, x_sample, state_gdn_fwd, state_gdn_bwd, cache_mla_ckv, cache_mla_krope, cache_gqa_k, cache_gqa_v, c, c_ctx, norm_mix, norm_mlp, w_mod, b_mod, w_mlp_in, w_mlp_out, gdn_w_in, gdn_conv, gdn_a_log, gdn_dt_bias, gdn_out_norm, gdn_w_out, mla_w_down, mla_q_lat_norm, mla_kv_lat_norm, mla_w_uq, mla_w_ukv, mla_qn_nope, mla_qn_rope, mla_kn_nope, mla_kn_rope, mla_w_out, gqa_w_in, gqa_q_norm, gqa_k_norm, gqa_w_out):
    bp, sp, d = x_prompt.shape
    bs, ss, _ = x_sample.shape
    past = cache_mla_ckv.shape[2]
    xs = [x_prompt.reshape(bp * sp, d), x_sample.reshape(bs * ss, d)]
    batches, seqs = (bp, bs), (sp, ss)
    rows_per_batch = (None, ss)
    tms = (TM_TOKENS, TM_TOKENS)

    cond = jnp.zeros((MOD_ROWS, d), F32).at[:bs].set(c).at[CTX_ROW].set(c_ctx)
    mods = _adaln(cond, w_mod, b_mod).reshape(DEPTH, MOD_ROWS, N_MOD, d)
    mods = jnp.pad(mods, ((0, 0), (0, 0), (0, MOD_PAD - N_MOD), (0, 0)))

    gqa_tables = _rope_tables(ss, HD_C, HD_C)
    mla_tables = _rope_tables(ss, ROPE_B, LANE)

    w_mi, w_mo = w_mlp_in.astype(BF16), w_mlp_out.astype(BF16)
    new_gdn_f, new_gdn_b, new_ckv, new_kr, new_k, new_v = [], [], [], [], [], []
    for i in range(DEPTH):
        kind, j = i % N_MIXERS, i // N_MIXERS
        mod = mods[i]
        g_mix = norm_mix[i].reshape(1, d)
        g_mlp = norm_mlp[i].reshape(1, d)
        mixed = []
        if kind == 0:
            w_in = gdn_w_in[j]
            n_main = 2 * WK_A + 2 * WV_A
            w_main = w_in[:, :n_main].astype(BF16)
            w_gate = _pad_cols(w_in[:, n_main:], LANE).astype(BF16)
            w_o = gdn_w_out[j].astype(BF16)
            on = gdn_out_norm[j].reshape(1, DV_A)
            for gi in range(2):
                b, s, tm = batches[gi], seqs[gi], tms[gi]
                proj, gb = _normproj(xs[gi], mod, g_mix, w_main, rows_per_batch[gi], tm, TN_PROJ, BF16, w_gate)
                gates = gb[:, :4 * H_A].reshape(b, s // CHUNK, CHUNK, 4 * H_A).transpose(0, 3, 1, 2)
                if gi == 0:
                    s0f = s0b = jnp.zeros((b, H_A, DK_A, DV_A), F32)
                else:
                    s0f, s0b = state_gdn_fwd[:, j], state_gdn_bwd[:, j]
                o, s_f, s_b = _gdn(proj, gates, gdn_conv[j], gdn_a_log[j], gdn_dt_bias[j], on, s0f, s0b, b, s)
                if gi == 0:
                    new_gdn_f.append(s_f)
                    new_gdn_b.append(s_b)
                mixed.append((o, w_o))
        elif kind == 1:
            w_uq = mla_w_uq[j].reshape(Q_LORA, H_B, NOPE_B + ROPE_B)
            w_uq = jnp.pad(w_uq, ((0, 0), (0, 0), (0, SLOT_B - NOPE_B - ROPE_B))).reshape(Q_LORA, H_B * SLOT_B)
            p = dict(q_lat_g=mla_q_lat_norm[j].reshape(1, Q_LORA), kv_lat_g=mla_kv_lat_norm[j].reshape(1, KV_LORA),
                     kn_rope=_pad_cols(mla_kn_rope[j].reshape(1, ROPE_B), LANE),
                     qn_rope=_pad_cols(mla_qn_rope[j].reshape(1, ROPE_B), LANE),
                     qn_nope=mla_qn_nope[j].reshape(1, NOPE_B), kn_nope=mla_kn_nope[j].reshape(1, NOPE_B),
                     w_uq=w_uq.astype(BF16), w_ukv=mla_w_ukv[j].astype(BF16))
            w_down = _pad_cols(mla_w_down[j], DOWN_PAD).astype(BF16)
            w_o = mla_w_out[j].astype(BF16)
            for gi in range(2):
                b, s, tm = batches[gi], seqs[gi], tms[gi]
                proj = _normproj(xs[gi], mod, g_mix, w_down, rows_per_batch[gi], tm, DOWN_PAD)
                q, k, v, ckv, kr = _mla_prep(proj, p, mla_tables if gi == 1 else None, s, TM_PREP)
                ctx = None
                if gi == 0:
                    new_ckv.append(ckv.reshape(b, s, KV_LORA))
                    new_kr.append(kr[:, :ROPE_B].reshape(b, s, ROPE_B))
                else:
                    ckv_c = cache_mla_ckv[:, j].reshape(b * past, KV_LORA)
                    kr_c = _pad_cols(cache_mla_krope[:, j].reshape(b * past, ROPE_B), LANE)
                    ctx = _mla_ctx(ckv_c, kr_c, p, TM_PREP)
                o = _attention(q, k, v, ctx, batch=b, seq=s, heads_kv=H_B, kv_per_step=4, group=1, dk=SLOT_B,
                               dv=V_B, tq=TQ_ATTN)
                mixed.append((o, w_o))
        else:
            w_in = gqa_w_in[j].astype(BF16)
            w_o = gqa_w_out[j].astype(BF16)
            q_g, k_g = gqa_q_norm[j].reshape(1, HD_C), gqa_k_norm[j].reshape(1, HD_C)
            for gi in range(2):
                b, s, tm = batches[gi], seqs[gi], tms[gi]
                q, k, v, kf, vf = _gqa_prep(xs[gi], mod, g_mix, w_in, q_g, k_g, gqa_tables if gi == 1 else None,
                                            rows_per_batch[gi], s, TM_PREP)
                ctx = None
                if gi == 0:
                    new_k.append(kf.reshape(b, s, KVH_C, HD_C))
                    new_v.append(vf.reshape(b, s, KVH_C, HD_C))
                else:
                    ctx = (cache_gqa_k[:, j].reshape(b * past, KVH_C * HD_C).astype(BF16),
                           cache_gqa_v[:, j].reshape(b * past, KVH_C * HD_C).astype(BF16))
                o = _attention(q, k, v, ctx, batch=b, seq=s, heads_kv=KVH_C, kv_per_step=2, group=H_C // KVH_C,
                               dk=HD_C, dv=HD_C, tq=TQ_ATTN)
                mixed.append((o, w_o))
        for gi in range(2):
            o, w_o = mixed[gi]
            xs[gi] = _mix_mlp(xs[gi], o, mod, g_mlp, w_o, w_mi, w_mo, i, rows_per_batch[gi], tms[gi], TF_MLP)

    dt = x_prompt.dtype
    stack = lambda lst: jnp.stack(lst, axis=1).astype(dt)
    return (xs[0].reshape(bp, sp, d), xs[1].reshape(bs, ss, d), stack(new_gdn_f), stack(new_gdn_b),
            stack(new_ckv), stack(new_kr), stack(new_k), stack(new_v))
```

```python
import functools
import math

import jax
import jax.numpy as jnp
from jax import lax
from jax.experimental import pallas as pl
from jax.experimental.pallas import tpu as pltpu

F32 = jnp.float32
BF16 = jnp.bfloat16

D_MODEL = 1024
DEPTH = 4
GRID_W = 64
N_MIXERS = 3
D_FF = 4 * D_MODEL
N_MOD = 6
EPS = 1e-6
ROPE_THETA = 10000.0
H_A = 8
DK_A = 128
DV_A = 128
WK_A = H_A * DK_A
WV_A = H_A * DV_A
CHUNK = 64
H_B = 8
Q_LORA = 384
KV_LORA = 256
NOPE_B = 128
ROPE_B = 64
V_B = 128
SLOT_B = 256
DOWN_PAD = 768
H_C = 8
KVH_C = 2
HD_C = 128

LOG2_E = math.log2(math.e)
LANE = 128
MOD_ROWS = 16
CTX_ROW = 8
MOD_PAD = 8
VMEM_LIMIT = 56 * 1024 * 1024
TM_TOKENS = 1024
TM_PREP = 512
TN_PROJ = 1024
TF_MLP = 1024
TN_ADALN = 1024
TQ_ATTN = 512


def _cparams(*sem):
    return pltpu.CompilerParams(dimension_semantics=sem, vmem_limit_bytes=VMEM_LIMIT)


def _rms(x, n):
    return x * lax.rsqrt(jnp.sum(x * x, axis=-1, keepdims=True) * (1.0 / n) + EPS)


def _dot(a, b):
    return jnp.dot(a, b, preferred_element_type=F32)


def _dot_nt(a, b):
    return lax.dot_general(a, b, (((1,), (1,)), ((), ())), preferred_element_type=F32)


def _dot_tn(a, b):
    return lax.dot_general(a, b, (((0,), (0,)), ((), ())), preferred_element_type=F32)


def _mod_index(rows_per_batch, tm):
    if rows_per_batch is None:
        return lambda t, *_: (CTX_ROW, 0, 0)
    return lambda t, *_: ((t * tm) // rows_per_batch, 0, 0)


def _adaln_kernel(c_ref, w_ref, b_ref, o_ref):
    c = c_ref[...]
    h = (c * jax.nn.sigmoid(c)).astype(BF16)
    o_ref[0] = _dot(h, w_ref[0].astype(BF16)) + b_ref[0]


def _adaln(cond, w_mod, b_mod):
    tn = TN_ADALN
    n = N_MOD * D_MODEL
    return pl.pallas_call(
        _adaln_kernel,
        grid=(DEPTH, n // tn),
        in_specs=[pl.BlockSpec((MOD_ROWS, D_MODEL), lambda i, j: (0, 0)),
                  pl.BlockSpec((1, D_MODEL, tn), lambda i, j: (i, 0, j)),
                  pl.BlockSpec((1, 1, tn), lambda i, j: (i, 0, j))],
        out_specs=pl.BlockSpec((1, MOD_ROWS, tn), lambda i, j: (i, 0, j)),
        out_shape=jax.ShapeDtypeStruct((DEPTH, MOD_ROWS, n), F32),
        compiler_params=_cparams("parallel", "parallel"),
        name="adaln",
    )(cond, w_mod, b_mod.reshape(DEPTH, 1, n))


def _modnorm(x, g, mod, shift_row, scale_row):
    y = _rms(x, D_MODEL) * g
    return y * (1.0 + mod[scale_row:scale_row + 1, :]) + mod[shift_row:shift_row + 1, :]


def _normproj_kernel(*refs, has_aux):
    if has_aux:
        x_ref, mod_ref, g_ref, w_ref, wa_ref, o_ref, oa_ref, h_ref = refs
    else:
        x_ref, mod_ref, g_ref, w_ref, o_ref, h_ref = refs

    j = pl.program_id(1)
    tn = o_ref.shape[1]

    @pl.when(j == 0)
    def _():
        h_ref[...] = _modnorm(x_ref[...], g_ref[...], mod_ref[0], 0, 1).astype(BF16)
        if has_aux:
            oa_ref[...] = _dot(h_ref[...], wa_ref[...])

    w = w_ref[:, pl.ds(pl.multiple_of(j * tn, tn), tn)]
    o_ref[...] = _dot(h_ref[...], w).astype(o_ref.dtype)


def _normproj(x, mod, g, w, rows_per_batch, tm, tn, out_dtype=F32, w_aux=None):
    t, n = x.shape[0], w.shape[1]
    in_specs = [pl.BlockSpec((tm, D_MODEL), lambda i, j: (i, 0)),
                pl.BlockSpec((1, MOD_PAD, D_MODEL), _mod_index(rows_per_batch, tm)),
                pl.BlockSpec((1, D_MODEL), lambda i, j: (0, 0)),
                pl.BlockSpec((D_MODEL, n), lambda i, j: (0, 0))]
    out_specs = [pl.BlockSpec((tm, tn), lambda i, j: (i, j))]
    out_shape = [jax.ShapeDtypeStruct((t, n), out_dtype)]
    args = [x, mod, g, w]
    if w_aux is not None:
        in_specs.append(pl.BlockSpec((D_MODEL, LANE), lambda i, j: (0, 0)))
        out_specs.append(pl.BlockSpec((tm, LANE), lambda i, j: (i, 0)))
        out_shape.append(jax.ShapeDtypeStruct((t, LANE), F32))
        args.append(w_aux)
    out = pl.pallas_call(
        functools.partial(_normproj_kernel, has_aux=w_aux is not None),
        grid=(t // tm, n // tn),
        in_specs=in_specs,
        out_specs=out_specs,
        out_shape=out_shape,
        scratch_shapes=[pltpu.VMEM((tm, D_MODEL), BF16)],
        compiler_params=_cparams("parallel", "arbitrary"),
        name="normproj",
    )(*args)
    return out if w_aux is not None else out[0]


def _mix_mlp_kernel(x_ref, a_ref, mod_ref, g_ref, wo_ref, win_ref, wout_ref, o_ref, h_ref, acc_ref):
    f = pl.program_id(1)

    @pl.when(f == 0)
    def _():
        o_ref[...] = x_ref[...] + mod_ref[0, 2:3, :] * _dot(a_ref[...], wo_ref[...])
        h_ref[...] = _modnorm(o_ref[...], g_ref[...], mod_ref[0], 3, 4).astype(BF16)
        acc_ref[...] = jnp.zeros_like(acc_ref)

    a = jnp.maximum(_dot(h_ref[...], win_ref[...]), 0.0)
    acc_ref[...] += _dot((a * a).astype(BF16), wout_ref[...])

    @pl.when(f == pl.num_programs(1) - 1)
    def _():
        o_ref[...] += mod_ref[0, 5:6, :] * acc_ref[...]


def _mix_mlp(x, a, mod, g, w_o, w_in, w_out, layer, rows_per_batch, tm, tf):
    t = x.shape[0]
    return pl.pallas_call(
        _mix_mlp_kernel,
        grid=(t // tm, D_FF // tf),
        in_specs=[pl.BlockSpec((tm, D_MODEL), lambda i, f: (i, 0)),
                  pl.BlockSpec((tm, D_MODEL), lambda i, f: (i, 0)),
                  pl.BlockSpec((1, MOD_PAD, D_MODEL), _mod_index(rows_per_batch, tm)),
                  pl.BlockSpec((1, D_MODEL), lambda i, f: (0, 0)),
                  pl.BlockSpec((D_MODEL, D_MODEL), lambda i, f: (0, 0)),
                  pl.BlockSpec((None, D_MODEL, tf), lambda i, f: (layer, 0, f)),
                  pl.BlockSpec((None, tf, D_MODEL), lambda i, f: (layer, f, 0))],
        out_specs=pl.BlockSpec((tm, D_MODEL), lambda i, f: (i, 0)),
        out_shape=jax.ShapeDtypeStruct((t, D_MODEL), F32),
        scratch_shapes=[pltpu.VMEM((tm, D_MODEL), BF16), pltpu.VMEM((tm, D_MODEL), F32)],
        compiler_params=_cparams("parallel", "arbitrary"),
        name="mix_mlp",
    )(x, a, mod, g, w_o, w_in, w_out)


def _attn_kernel(*refs, kv_per_step, group, dk, dv, has_ctx):
    if has_ctx:
        q_ref, k_ref, v_ref, kc_ref, vc_ref, o_ref = refs
    else:
        q_ref, k_ref, v_ref, o_ref = refs

    def scores(n):
        j = n // group
        q = q_ref[:, n * dk:(n + 1) * dk]
        s = _dot_nt(q, k_ref[:, j * dk:(j + 1) * dk])
        sc = _dot_nt(q, kc_ref[:, j * dk:(j + 1) * dk]) if has_ctx else None
        return s, sc

    heads = kv_per_step * group
    nxt = scores(0)
    for n in range(heads):
        s, sc = nxt
        if n + 1 < heads:
            nxt = scores(n + 1)
        j = n // group
        m = jnp.max(s, axis=-1, keepdims=True)
        if has_ctx:
            m = jnp.maximum(m, jnp.max(sc, axis=-1, keepdims=True))
            pc = jnp.exp2(sc - m)
        p = jnp.exp2(s - m)
        l = jnp.sum(p, axis=-1, keepdims=True)
        o = _dot(p.astype(BF16), v_ref[:, j * dv:(j + 1) * dv])
        if has_ctx:
            l = l + jnp.sum(pc, axis=-1, keepdims=True)
            o = o + _dot(pc.astype(BF16), vc_ref[:, j * dv:(j + 1) * dv])
        o_ref[:, n * dv:(n + 1) * dv] = (o / l).astype(o_ref.dtype)


def _attention(q, k, v, ctx, *, batch, seq, heads_kv, kv_per_step, group, dk, dv, tq):
    tq = min(tq, seq)
    nq = seq // tq
    hp = kv_per_step
    in_specs = [pl.BlockSpec((tq, hp * group * dk), lambda b, j, i: (b * nq + i, j)),
                pl.BlockSpec((seq, hp * dk), lambda b, j, i: (b, j)),
                pl.BlockSpec((seq, hp * dv), lambda b, j, i: (b, j))]
    args = [q, k, v]
    if ctx is not None:
        past = ctx[0].shape[0] // batch
        in_specs += [pl.BlockSpec((past, hp * dk), lambda b, j, i: (b, j)),
                     pl.BlockSpec((past, hp * dv), lambda b, j, i: (b, j))]
        args += list(ctx)
    return pl.pallas_call(
        functools.partial(_attn_kernel, kv_per_step=hp, group=group, dk=dk, dv=dv, has_ctx=ctx is not None),
        grid=(batch, heads_kv // hp, nq),
        in_specs=in_specs,
        out_specs=pl.BlockSpec((tq, hp * group * dv), lambda b, j, i: (b * nq + i, j)),
        out_shape=jax.ShapeDtypeStruct((batch * seq, heads_kv * group * dv), BF16),
        compiler_params=_cparams("parallel", "parallel", "parallel"),
        name="attention",
    )(*args)


def _swap_quarters(x, quarter):
    lanes = x.shape[-1]
    lane = lax.broadcasted_iota(jnp.int32, x.shape, x.ndim - 1)
    up = pltpu.roll(x, lanes - quarter, x.ndim - 1)
    down = pltpu.roll(x, quarter, x.ndim - 1)
    return jnp.where((lane // quarter) % 2 == 0, up, down)


def _rope(x, cos, sin, quarter):
    return x * cos + _swap_quarters(x, quarter) * sin


def _rope_tables(seq, width, pad_to):
    rows = seq // GRID_W
    row = jnp.repeat(jnp.arange(rows, dtype=F32), GRID_W)
    col = jnp.tile(jnp.arange(GRID_W, dtype=F32), rows)
    quarter = width // 4
    freqs = ROPE_THETA ** (-jnp.arange(quarter, dtype=F32) / quarter)
    ar = row[:, None] * freqs[None, :]
    ac = col[:, None] * freqs[None, :]
    cos = jnp.concatenate([jnp.cos(ar), jnp.cos(ar), jnp.cos(ac), jnp.cos(ac)], axis=-1)
    sin = jnp.concatenate([-jnp.sin(ar), jnp.sin(ar), -jnp.sin(ac), jnp.sin(ac)], axis=-1)
    pad = ((0, 0), (0, pad_to - width))
    return jnp.pad(cos, pad), jnp.pad(sin, pad)


def _gqa_prep_kernel(*refs, use_rope):
    if use_rope:
        x_ref, mod_ref, g_ref, w_ref, qg_ref, kg_ref, cos_ref, sin_ref, q_ref, k_ref, v_ref, kf_ref, vf_ref = refs
    else:
        x_ref, mod_ref, g_ref, w_ref, qg_ref, kg_ref, q_ref, k_ref, v_ref, kf_ref, vf_ref = refs
    proj = _dot(_modnorm(x_ref[...], g_ref[...], mod_ref[0], 0, 1).astype(BF16), w_ref[...])
    for h in range(H_C + KVH_C):
        x = _rms(proj[:, h * HD_C:(h + 1) * HD_C], HD_C)
        x = x * (qg_ref[...] if h < H_C else kg_ref[...])
        if h >= H_C:
            kf_ref[:, (h - H_C) * HD_C:(h - H_C + 1) * HD_C] = x
        if use_rope:
            x = _rope(x, cos_ref[...], sin_ref[...], HD_C // 4)
        if h < H_C:
            q_ref[:, h * HD_C:(h + 1) * HD_C] = (x * (LOG2_E * HD_C ** -0.5)).astype(BF16)
        else:
            k_ref[:, (h - H_C) * HD_C:(h - H_C + 1) * HD_C] = x.astype(BF16)
    vv = proj[:, (H_C + KVH_C) * HD_C:]
    vf_ref[...] = vv
    v_ref[...] = vv.astype(BF16)


def _gqa_prep(x, mod, g, w, q_g, k_g, tables, rows_per_batch, seq, tm):
    t = x.shape[0]
    nq, nkv = H_C * HD_C, KVH_C * HD_C
    in_specs = [pl.BlockSpec((tm, D_MODEL), lambda i: (i, 0)),
                pl.BlockSpec((1, MOD_PAD, D_MODEL), _mod_index(rows_per_batch, tm)),
                pl.BlockSpec((1, D_MODEL), lambda i: (0, 0)),
                pl.BlockSpec((D_MODEL, nq + 2 * nkv), lambda i: (0, 0)),
                pl.BlockSpec((1, HD_C), lambda i: (0, 0)),
                pl.BlockSpec((1, HD_C), lambda i: (0, 0))]
    args = [x, mod, g, w, q_g, k_g]
    if tables is not None:
        per = seq // tm
        in_specs += [pl.BlockSpec((tm, HD_C), lambda i: (i % per, 0))] * 2
        args += list(tables)
    return pl.pallas_call(
        functools.partial(_gqa_prep_kernel, use_rope=tables is not None),
        grid=(t // tm,),
        in_specs=in_specs,
        out_specs=[pl.BlockSpec((tm, nq), lambda i: (i, 0))] + [pl.BlockSpec((tm, nkv), lambda i: (i, 0))] * 4,
        out_shape=[jax.ShapeDtypeStruct((t, nq), BF16), jax.ShapeDtypeStruct((t, nkv), BF16),
                   jax.ShapeDtypeStruct((t, nkv), BF16), jax.ShapeDtypeStruct((t, nkv), F32),
                   jax.ShapeDtypeStruct((t, nkv), F32)],
        compiler_params=_cparams("parallel"),
        name="gqa_prep",
    )(*args)


def _mla_keys_values(ckv, krope_slot, wukv_ref, knn_ref, k_ref, v_ref):
    kv = _dot(ckv.astype(BF16), wukv_ref[...])
    kr = krope_slot.astype(BF16)
    for h in range(H_B):
        base = h * (NOPE_B + V_B)
        kn = _rms(kv[:, base:base + NOPE_B], NOPE_B) * knn_ref[...]
        k_ref[:, h * SLOT_B:h * SLOT_B + NOPE_B] = kn.astype(BF16)
        k_ref[:, h * SLOT_B + NOPE_B:(h + 1) * SLOT_B] = kr
        v_ref[:, h * V_B:(h + 1) * V_B] = kv[:, base + NOPE_B:base + NOPE_B + V_B].astype(BF16)


def _mla_prep_kernel(*refs, use_rope):
    if use_rope:
        (p_ref, qlg_ref, kvg_ref, krg_ref, wuq_ref, wukv_ref, qnn_ref, qnr_ref, knn_ref, cos_ref, sin_ref,
         q_ref, k_ref, v_ref, ckv_ref, kr_ref) = refs
    else:
        (p_ref, qlg_ref, kvg_ref, krg_ref, wuq_ref, wukv_ref, qnn_ref, qnr_ref, knn_ref,
         q_ref, k_ref, v_ref, ckv_ref, kr_ref) = refs
    cq = _rms(p_ref[:, :Q_LORA], Q_LORA) * qlg_ref[...]
    ckv = _rms(p_ref[:, Q_LORA:Q_LORA + KV_LORA], KV_LORA) * kvg_ref[...]
    kr = _rms(p_ref[:, Q_LORA + KV_LORA:], ROPE_B) * krg_ref[...]
    ckv_ref[...] = ckv
    kr_ref[...] = kr
    q = _dot(cq.astype(BF16), wuq_ref[...])
    scale = LOG2_E * (NOPE_B + ROPE_B) ** -0.5
    for h in range(H_B):
        qn = _rms(q[:, h * SLOT_B:h * SLOT_B + NOPE_B], NOPE_B) * qnn_ref[...]
        qr = _rms(q[:, h * SLOT_B + NOPE_B:(h + 1) * SLOT_B], ROPE_B) * qnr_ref[...]
        if use_rope:
            qr = _rope(qr, cos_ref[...], sin_ref[...], ROPE_B // 4)
        q_ref[:, h * SLOT_B:h * SLOT_B + NOPE_B] = (qn * scale).astype(BF16)
        q_ref[:, h * SLOT_B + NOPE_B:(h + 1) * SLOT_B] = (qr * scale).astype(BF16)
    if use_rope:
        kr = _rope(kr, cos_ref[...], sin_ref[...], ROPE_B // 4)
    _mla_keys_values(ckv, kr, wukv_ref, knn_ref, k_ref, v_ref)


def _mla_prep(proj, p, tables, seq, tm):
    t = proj.shape[0]
    full = lambda shape: pl.BlockSpec(shape, lambda i: (0, 0))
    in_specs = [pl.BlockSpec((tm, DOWN_PAD), lambda i: (i, 0)),
                full((1, Q_LORA)), full((1, KV_LORA)), full((1, LANE)),
                full((Q_LORA, H_B * SLOT_B)), full((KV_LORA, H_B * (NOPE_B + V_B))),
                full((1, NOPE_B)), full((1, LANE)), full((1, NOPE_B))]
    args = [proj, p["q_lat_g"], p["kv_lat_g"], p["kn_rope"], p["w_uq"], p["w_ukv"], p["qn_nope"], p["qn_rope"],
            p["kn_nope"]]
    if tables is not None:
        per = seq // tm
        in_specs += [pl.BlockSpec((tm, LANE), lambda i: (i % per, 0))] * 2
        args += list(tables)
    row = lambda n: pl.BlockSpec((tm, n), lambda i: (i, 0))
    return pl.pallas_call(
        functools.partial(_mla_prep_kernel, use_rope=tables is not None),
        grid=(t // tm,),
        in_specs=in_specs,
        out_specs=[row(H_B * SLOT_B), row(H_B * SLOT_B), row(H_B * V_B), row(KV_LORA), row(LANE)],
        out_shape=[jax.ShapeDtypeStruct((t, H_B * SLOT_B), BF16), jax.ShapeDtypeStruct((t, H_B * SLOT_B), BF16),
                   jax.ShapeDtypeStruct((t, H_B * V_B), BF16), jax.ShapeDtypeStruct((t, KV_LORA), F32),
                   jax.ShapeDtypeStruct((t, LANE), F32)],
        compiler_params=_cparams("parallel"),
        name="mla_prep",
    )(*args)


def _mla_ctx_kernel(ckv_ref, kr_ref, wukv_ref, knn_ref, k_ref, v_ref):
    _mla_keys_values(ckv_ref[...], kr_ref[...], wukv_ref, knn_ref, k_ref, v_ref)


def _mla_ctx(ckv, krope_slot, p, tm):
    t = ckv.shape[0]
    full = lambda shape: pl.BlockSpec(shape, lambda i: (0, 0))
    row = lambda n: pl.BlockSpec((tm, n), lambda i: (i, 0))
    return pl.pallas_call(
        _mla_ctx_kernel,
        grid=(t // tm,),
        in_specs=[row(KV_LORA), row(LANE), full((KV_LORA, H_B * (NOPE_B + V_B))), full((1, NOPE_B))],
        out_specs=[row(H_B * SLOT_B), row(H_B * V_B)],
        out_shape=[jax.ShapeDtypeStruct((t, H_B * SLOT_B), BF16), jax.ShapeDtypeStruct((t, H_B * V_B), BF16)],
        compiler_params=_cparams("parallel"),
        name="mla_ctx",
    )(ckv, krope_slot, p["w_ukv"], p["kn_nope"])


def _silu(y):
    h = 0.5 * y
    return h * jnp.tanh(h) + h


def _conv_silu(x, w):
    s = x.shape[0]
    row = lax.broadcasted_iota(jnp.int32, x.shape, 0)
    prev = jnp.where(row == 0, 0.0, pltpu.roll(x, 1, 0))
    nxt = jnp.where(row == s - 1, 0.0, pltpu.roll(x, s - 1, 0))
    y = prev * w[0:1, :] + x * w[1:2, :] + nxt * w[2:3, :]
    return _silu(y)


def _l2n(x):
    return x * lax.rsqrt(jnp.sum(x * x, axis=-1, keepdims=True) + EPS)


def _softplus(x):
    return jnp.maximum(x, 0.0) + jnp.log1p(jnp.exp(-jnp.abs(x)))


GDN_CHUNK_HEADS = 64
GDN_PACK = 4
GDN_UNROLL = 16


def _gdn_kernel(alog_ref, dtb_ref, q_ref, k_ref, v_ref, z_ref, cq_ref, ck_ref, cv_ref, gate_ref,
                s0f_ref, s0b_ref, on_ref, o_ref, sf_ref, sb_ref,
                qs, ks, vs, obuf, qeff, nmat, mneg, egt, st, *, seq, heads, unroll, pack):
    h0 = pl.program_id(1) * heads
    n_chunks = seq // CHUNK
    for hh in range(heads):
        cols = slice(hh * DK_A, (hh + 1) * DK_A)
        qs[hh] = _l2n(_conv_silu(q_ref[:, cols].astype(F32), cq_ref[:, cols])) * (DK_A ** -0.5)
        ks[hh] = _l2n(_conv_silu(k_ref[:, cols].astype(F32), ck_ref[:, cols]))
        vs[hh] = _conv_silu(v_ref[:, cols].astype(F32), cv_ref[:, cols])

    width = pack * CHUNK
    ii = lax.broadcasted_iota(jnp.int32, (CHUNK, CHUNK), 0)
    jj = lax.broadcasted_iota(jnp.int32, (CHUNK, CHUNK), 1)
    eye = ii == jj
    ip = lax.broadcasted_iota(jnp.int32, (CHUNK, width), 0)
    lane = lax.broadcasted_iota(jnp.int32, (CHUNK, width), 1)
    jp = lane % CHUNK
    in_blk = [(lane // CHUNK) == i for i in range(pack)]
    eye_p = ip == jp
    incl = (ii >= jj, ii <= jj)
    incl_p = (ip >= jp, ip <= jp)
    strict_p = (ip > jp, ip < jp)
    last = (CHUNK - 1, 0)
    pair_p = [(ip // 2) == (jp // 2)]
    pair_p += [((ip // (2 * s)) == (jp // (2 * s))) & ((ip // s) != (jp // s)) for s in (2, 4, 8, 16, 32)]
    slots = [(hh, d) for hh in range(heads) for d in range(2)]

    def pack_cols(cols):
        out = jnp.broadcast_to(cols[0], (CHUNK, width))
        for i in range(1, pack):
            out = jnp.where(in_blk[i], cols[i], out)
        return out

    def pack_diag(g):
        out = g[:CHUNK, :]
        for i in range(1, pack):
            out = jnp.where(in_blk[i], g[i * CHUNK:(i + 1) * CHUNK, :], out)
        return out

    def block_diag(x):
        return jnp.concatenate([jnp.where(in_blk[i], x, jnp.zeros_like(x)) for i in range(pack)], axis=0)
    neg_a = [-jnp.exp(jnp.full((1, CHUNK), alog_ref[d, h0 + hh], F32)) for hh, d in slots]
    dtb = [dtb_ref[d, h0 + hh] for hh, d in slots]

    def to_col(r):
        return jnp.sum(jnp.where(eye, jnp.broadcast_to(r, (CHUNK, CHUNK)), 0.0), axis=1, keepdims=True)

    def prepare(blk, carry):
        c0s = [blk * unroll + p * pack for p in range(unroll // pack)]
        offs = [pl.multiple_of(c0 * CHUNK, width) for c0 in c0s]
        tiles = [(hh, p) for hh in range(heads) for p in range(len(c0s))]
        qt = {t: qs[t[0], pl.ds(offs[t[1]], width), :] for t in tiles}
        kt = {t: ks[t[0], pl.ds(offs[t[1]], width), :] for t in tiles}
        vt = {t: vs[t[0], pl.ds(offs[t[1]], width), :] for t in tiles}
        ktb = {t: kt[t].astype(BF16) for t in tiles}
        kk = {t: pack_diag(_dot_nt(ktb[t], ktb[t])) for t in tiles}
        qk_raw = {t: pack_diag(_dot_nt(qt[t].astype(BF16), ktb[t])) for t in tiles}
        groups = [(hh, p, d) for hh, p in tiles for d in range(2)]
        a_mat, qk, beta_c, gcum_c, g_tot = [], [], [], [], []
        for hh, p, d in groups:
            slot = 2 * hh + d
            cols_beta, cols_gcum, tot = [], [], []
            for i in range(pack):
                g_raw = gate_ref[0, d * H_A + h0 + hh, pl.ds(c0s[p] + i, 1), :]
                b_raw = gate_ref[0, 2 * H_A + d * H_A + h0 + hh, pl.ds(c0s[p] + i, 1), :]
                g_row = neg_a[slot] * _softplus(g_raw + dtb[slot])
                gcum = jnp.sum(jnp.where(incl[d], jnp.broadcast_to(g_row, (CHUNK, CHUNK)), 0.0), axis=1,
                               keepdims=True)
                cols_gcum.append(gcum)
                cols_beta.append(to_col(jax.nn.sigmoid(b_raw)))
                tot.append(gcum[last[d]:last[d] + 1, :])
            gcum_p = pack_cols(cols_gcum)
            gcum_row = jnp.sum(jnp.where(eye_p, gcum_p, 0.0), axis=0, keepdims=True)
            decay = jnp.exp(jnp.where(incl_p[d], gcum_p - gcum_row, -jnp.inf))
            a_mat.append(jnp.where(strict_p[d], kk[hh, p] * pack_cols(cols_beta) * decay, 0.0))
            qk.append((qk_raw[hh, p] * decay).astype(BF16))
            beta_c.append(cols_beta)
            gcum_c.append(cols_gcum)
            g_tot.append(tot)
        tm = [jnp.where(eye_p, 1.0, -jnp.where(pair_p[0], a, 0.0)) for a in a_mat]
        for lvl in range(1, len(pair_p)):
            tb = [t.astype(BF16) for t in tm]
            x = [_dot(jnp.where(pair_p[lvl], a, 0.0).astype(BF16), block_diag(t)) for a, t in zip(a_mat, tb)]
            tm = [t - _dot(t16, block_diag(xi.astype(BF16))) for t, t16, xi in zip(tm, tb, x)]
        nm = [jnp.where(eye_p, 0.0, t).astype(BF16) for t in tm]
        rows = [slice(i * CHUNK, (i + 1) * CHUNK) for i in range(pack)]
        eg_c = [[jnp.exp(x) for x in cols] for cols in gcum_c]
        rhs = [jnp.concatenate(
            [jnp.concatenate([vt[hh, p][rows[i]] * beta_c[g][i], kt[hh, p][rows[i]] * (beta_c[g][i] * eg_c[g][i])],
                             axis=-1) for i in range(pack)], axis=0)
            for g, (hh, p, d) in enumerate(groups)]
        rhsb = [r.astype(BF16) for r in rhs]
        zero_p = jnp.zeros((CHUNK, width), BF16)
        sol = [jnp.concatenate([(rhs[g][rows[i]] + _dot(jnp.where(in_blk[i], nm[g], zero_p), rhsb[g])).astype(BF16)
                                for i in range(pack)], axis=0) for g in range(len(groups))]
        local = [jnp.concatenate([_dot(jnp.where(in_blk[i], qk[g], zero_p), sol[g]) for i in range(pack)], axis=0)
                 for g in range(len(groups))]
        k_dec = [[(kt[hh, p][rows[i]] * jnp.exp(g_tot[g][i] - gcum_c[g][i])).astype(BF16) for i in range(pack)]
                 for g, (hh, p, d) in enumerate(groups)]
        nm_mat = [[_dot_tn(k_dec[g][i], sol[g][rows[i]]) for i in range(pack)]
                  for g in range(len(groups))]
        for g, (hh, p, d) in enumerate(groups):
            slot = 2 * hh + d
            obuf[slot, pl.ds(offs[p], width), :] = local[g][:, :DV_A]
            q_eff = [qt[hh, p][rows[i]] * eg_c[g][i] - local[g][rows[i], DV_A:] for i in range(pack)]
            qeff[slot, pl.ds(offs[p], width), :] = jnp.concatenate(q_eff, axis=0).astype(BF16)
            for i in range(pack):
                nmat[slot, c0s[p] + i] = nm_mat[g][i][:, :DV_A]
                mneg[slot, c0s[p] + i] = (-nm_mat[g][i][:, DV_A:]).astype(BF16)
                egt[slot, c0s[p] + i] = jnp.broadcast_to(jnp.exp(g_tot[g][i]), (8, DV_A))
        return carry

    lax.fori_loop(0, n_chunks // unroll, prepare, 0)

    for hh, d in slots:
        st[2 * hh + d] = (s0f_ref, s0b_ref)[d][0, hh]

    def scan(ci, carry):
        states = [st[slot] for slot in range(len(slots))]
        sb16 = [s.astype(BF16) for s in states]
        cs = [ci if d == 0 else n_chunks - 1 - ci for hh, d in slots]
        for slot, c in enumerate(cs):
            st[slot] = states[slot] * egt[slot, c][0:1, :] + _dot(mneg[slot, c], sb16[slot]) + nmat[slot, c]
        for slot, c in enumerate(cs):
            off = pl.multiple_of(c * CHUNK, CHUNK)
            obuf[slot, pl.ds(off, CHUNK), :] += _dot(qeff[slot, pl.ds(off, CHUNK), :], sb16[slot])
        return carry

    lax.fori_loop(0, n_chunks, scan, 0, unroll=8 if n_chunks % 8 == 0 else 4)
    for hh, d in slots:
        (sf_ref, sb_ref)[d][0, hh] = st[2 * hh + d]

    for hh in range(heads):
        cols = slice(hh * DV_A, (hh + 1) * DV_A)
        z =z_ref[:, cols].astype(F32)
        o = _rms(obuf[2 * hh] + obuf[2 * hh + 1], DV_A) * on_ref[...] * _silu(z)
        o_ref[:, cols] = o.astype(BF16)


def _gdn(proj, gates, conv_w, a_log, dt_bias, out_norm, s0f, s0b, batch, seq):
    n_chunks = seq // CHUNK
    unroll = math.gcd(n_chunks, GDN_UNROLL)
    pack = math.gcd(unroll, GDN_PACK if unroll > GDN_PACK else GDN_PACK // 2)
    hb = min(4, max(2, GDN_CHUNK_HEADS // n_chunks))
    groups, slots = H_A // hb, 2 * hb
    smem = pl.BlockSpec(memory_space=pltpu.SMEM)
    col = lambda off: pl.BlockSpec((seq, hb * DK_A), lambda b, h, off=off: (b, off + h))
    cw = lambda off: pl.BlockSpec((3, hb * DK_A), lambda b, h, off=off: (0, off + h))
    state = pl.BlockSpec((1, hb, DK_A, DV_A), lambda b, h: (b, h, 0, 0))
    st_shape = jax.ShapeDtypeStruct((batch, H_A, DK_A, DV_A), F32)
    return pl.pallas_call(
        functools.partial(_gdn_kernel, seq=seq, heads=hb, unroll=unroll, pack=pack),
        grid=(batch, groups),
        in_specs=[smem, smem, col(0), col(groups), col(2 * groups), col(3 * groups),
                  cw(0), cw(groups), cw(2 * groups),
                  pl.BlockSpec((1, 4 * H_A, n_chunks, CHUNK), lambda b, h: (b, 0, 0, 0)),
                  state, state, pl.BlockSpec((1, DV_A), lambda b, h: (0, 0))],
        out_specs=[pl.BlockSpec((seq, hb * DV_A), lambda b, h: (b, h)), state, state],
        out_shape=[jax.ShapeDtypeStruct((batch * seq, WV_A), BF16), st_shape, st_shape],
        scratch_shapes=[pltpu.VMEM((hb, seq, DK_A), F32), pltpu.VMEM((hb, seq, DK_A), F32),
                        pltpu.VMEM((hb, seq, DV_A), F32),
                        pltpu.VMEM((slots, seq, DV_A), F32), pltpu.VMEM((slots, seq, DK_A), BF16),
                        pltpu.VMEM((slots, n_chunks, DK_A, DV_A), F32),
                        pltpu.VMEM((slots, n_chunks, DK_A, DK_A), BF16),
                        pltpu.VMEM((slots, n_chunks, 8, DV_A), F32), pltpu.VMEM((slots, DK_A, DV_A), F32)],
        compiler_params=_cparams("parallel", "parallel"),
        name="gdn",
    )(a_log, dt_bias, proj, proj, proj, proj, conv_w, conv_w, conv_w, gates, s0f, s0b, out_norm)


def _pad_cols(w, n):
    return jnp.pad(w, ((0, 0), (0, n - w.shape[1])))


def kernel(x_prompt, x_sample, state_gdn_fwd, state_gdn_bwd, cache_mla_ckv, cache_mla_krope, cache_gqa_k, cache_gqa_v, c, c_ctx, norm_mix, norm_mlp, w_mod, b_mod, w_mlp_in, w_mlp_out, gdn_w_in, gdn_conv, gdn_a_log, gdn_dt_bias, gdn_out_norm, gdn_w_out, mla_w_down, mla_q_lat_norm, mla_kv_lat_norm, mla_w_uq, mla_w_ukv, mla_qn_nope, mla_qn_rope, mla_kn_nope, mla_kn_rope, mla_w_out, gqa_w_in, gqa_q_norm, gqa_k_norm, gqa_w_out):
    bp, sp, d = x_prompt.shape
    bs, ss, _ = x_sample.shape
    past = cache_mla_ckv.shape[2]
    xs = [x_prompt.reshape(bp * sp, d), x_sample.reshape(bs * ss, d)]
    batches, seqs = (bp, bs), (sp, ss)
    rows_per_batch = (None, ss)
    tms = (TM_TOKENS, TM_TOKENS)

    cond = jnp.zeros((MOD_ROWS, d), F32).at[:bs].set(c).at[CTX_ROW].set(c_ctx)
    mods = _adaln(cond, w_mod, b_mod).reshape(DEPTH, MOD_ROWS, N_MOD, d)
    mods = jnp.pad(mods, ((0, 0), (0, 0), (0, MOD_PAD - N_MOD), (0, 0)))

    gqa_tables = _rope_tables(ss, HD_C, HD_C)
    mla_tables = _rope_tables(ss, ROPE_B, LANE)

    w_mi, w_mo = w_mlp_in.astype(BF16), w_mlp_out.astype(BF16)
    new_gdn_f, new_gdn_b, new_ckv, new_kr, new_k, new_v = [], [], [], [], [], []
    for i in range(DEPTH):
        kind, j = i % N_MIXERS, i // N_MIXERS
        mod = mods[i]
        g_mix = norm_mix[i].reshape(1, d)
        g_mlp = norm_mlp[i].reshape(1, d)
        mixed = []
        if kind == 0:
            w_in = gdn_w_in[j]
            n_main = 2 * WK_A + 2 * WV_A
            w_main = w_in[:, :n_main].astype(BF16)
            w_gate = _pad_cols(w_in[:, n_main:], LANE).astype(BF16)
            w_o = gdn_w_out[j].astype(BF16)
            on = gdn_out_norm[j].reshape(1, DV_A)
            for gi in range(2):
                b, s, tm = batches[gi], seqs[gi], tms[gi]
                proj, gb = _normproj(xs[gi], mod, g_mix, w_main, rows_per_batch[gi], tm, TN_PROJ, BF16, w_gate)
                gates = gb[:, :4 * H_A].reshape(b, s // CHUNK, CHUNK, 4 * H_A).transpose(0, 3, 1, 2)
                if gi == 0:
                    s0f = s0b = jnp.zeros((b, H_A, DK_A, DV_A), F32)
                else:
                    s0f, s0b = state_gdn_fwd[:, j], state_gdn_bwd[:, j]
                o, s_f, s_b = _gdn(proj, gates, gdn_conv[j], gdn_a_log[j], gdn_dt_bias[j], on, s0f, s0b, b, s)
                if gi == 0:
                    new_gdn_f.append(s_f)
                    new_gdn_b.append(s_b)
                mixed.append((o, w_o))
        elif kind == 1:
            w_uq = mla_w_uq[j].reshape(Q_LORA, H_B, NOPE_B + ROPE_B)
            w_uq = jnp.pad(w_uq, ((0, 0), (0, 0), (0, SLOT_B - NOPE_B - ROPE_B))).reshape(Q_LORA, H_B * SLOT_B)
            p = dict(q_lat_g=mla_q_lat_norm[j].reshape(1, Q_LORA), kv_lat_g=mla_kv_lat_norm[j].reshape(1, KV_LORA),
                     kn_rope=_pad_cols(mla_kn_rope[j].reshape(1, ROPE_B), LANE),
                     qn_rope=_pad_cols(mla_qn_rope[j].reshape(1, ROPE_B), LANE),
                     qn_nope=mla_qn_nope[j].reshape(1, NOPE_B), kn_nope=mla_kn_nope[j].reshape(1, NOPE_B),
                     w_uq=w_uq.astype(BF16), w_ukv=mla_w_ukv[j].astype(BF16))
            w_down = _pad_cols(mla_w_down[j], DOWN_PAD).astype(BF16)
            w_o = mla_w_out[j].astype(BF16)
            for gi in range(2):
                b, s, tm = batches[gi], seqs[gi], tms[gi]
                proj = _normproj(xs[gi], mod, g_mix, w_down, rows_per_batch[gi], tm, DOWN_PAD)
                q, k, v, ckv, kr = _mla_prep(proj, p, mla_tables if gi == 1 else None, s, TM_PREP)
                ctx = None
                if gi == 0:
                    new_ckv.append(ckv.reshape(b, s, KV_LORA))
                    new_kr.append(kr[:, :ROPE_B].reshape(b, s, ROPE_B))
                else:
                    ckv_c = cache_mla_ckv[:, j].reshape(b * past, KV_LORA)
                    kr_c = _pad_cols(cache_mla_krope[:, j].reshape(b * past, ROPE_B), LANE)
                    ctx = _mla_ctx(ckv_c, kr_c, p, TM_PREP)
                o = _attention(q, k, v, ctx, batch=b, seq=s, heads_kv=H_B, kv_per_step=4, group=1, dk=SLOT_B,
                               dv=V_B, tq=TQ_ATTN)
                mixed.append((o, w_o))
        else:
            w_in = gqa_w_in[j].astype(BF16)
            w_o = gqa_w_out[j].astype(BF16)
            q_g, k_g = gqa_q_norm[j].reshape(1, HD_C), gqa_k_norm[j].reshape(1, HD_C)
            for gi in range(2):
                b, s, tm = batches[gi], seqs[gi], tms[gi]
                q, k, v, kf, vf = _gqa_prep(xs[gi], mod, g_mix, w_in, q_g, k_g, gqa_tables if gi == 1 else None,
                                            rows_per_batch[gi], s, TM_PREP)
                ctx = None
                if gi == 0:
                    new_k.append(kf.reshape(b, s, KVH_C, HD_C))
                    new_v.append(vf.reshape(b, s, KVH_C, HD_C))
                else:
                    ctx = (cache_gqa_k[:, j].reshape(b * past, KVH_C * HD_C).astype(BF16),
                           cache_gqa_v[:, j].reshape(b * past, KVH_C * HD_C).astype(BF16))
                o = _attention(q, k, v, ctx, batch=b, seq=s, heads_kv=KVH_C, kv_per_step=2, group=H_C // KVH_C,
                               dk=HD_C, dv=HD_C, tq=TQ_ATTN)
                mixed.append((o, w_o))
        for gi in range(2):
            o, w_o = mixed[gi]
            xs[gi] = _mix_mlp(xs[gi], o, mod, g_mlp, w_o, w_mi, w_mo, i, rows_per_batch[gi], tms[gi], TF_MLP)

    dt = x_prompt.dtype
    stack = lambda lst: jnp.stack(lst, axis=1).astype(dt)
    return (xs[0].reshape(bp, sp, d), xs[1].reshape(bs, ss, d), stack(new_gdn_f), stack(new_gdn_b),
            stack(new_ckv), stack(new_kr), stack(new_k), stack(new_v))
```

```python
import functools
import math

import jax
import jax.numpy as jnp
from jax import lax
from jax.experimental import pallas as pl
from jax.experimental.pallas import tpu as pltpu

F32 = jnp.float32
BF16 = jnp.bfloat16

D_MODEL = 1024
DEPTH = 4
GRID_W = 64
N_MIXERS = 3
D_FF = 4 * D_MODEL
N_MOD = 6
EPS = 1e-6
ROPE_THETA = 10000.0
H_A = 8
DK_A = 128
DV_A = 128
WK_A = H_A * DK_A
WV_A = H_A * DV_A
CHUNK = 64
H_B = 8
Q_LORA = 384
KV_LORA = 256
NOPE_B = 128
ROPE_B = 64
V_B = 128
SLOT_B = 256
DOWN_PAD = 768
H_C = 8
KVH_C = 2
HD_C = 128

LOG2_E = math.log2(math.e)
LANE = 128
MOD_ROWS = 16
CTX_ROW = 8
MOD_PAD = 8
VMEM_LIMIT = 56 * 1024 * 1024
TM_TOKENS = 1024
TM_PREP = 512
TN_PROJ = 1024
TF_MLP = 1024
TN_ADALN = 1024
TQ_ATTN = 512


def _cparams(*sem):
    return pltpu.CompilerParams(dimension_semantics=sem, vmem_limit_bytes=VMEM_LIMIT)


def _rms(x, n):
    return x * lax.rsqrt(jnp.sum(x * x, axis=-1, keepdims=True) * (1.0 / n) + EPS)


def _dot(a, b):
    return jnp.dot(a, b, preferred_element_type=F32)


def _dot_nt(a, b):
    return lax.dot_general(a, b, (((1,), (1,)), ((), ())), preferred_element_type=F32)


def _dot_tn(a, b):
    return lax.dot_general(a, b, (((0,), (0,)), ((), ())), preferred_element_type=F32)


def _mod_index(rows_per_batch, tm):
    if rows_per_batch is None:
        return lambda t, *_: (CTX_ROW, 0, 0)
    return lambda t, *_: ((t * tm) // rows_per_batch, 0, 0)


def _adaln_kernel(c_ref, w_ref, b_ref, o_ref):
    c = c_ref[...]
    h = (c * jax.nn.sigmoid(c)).astype(BF16)
    o_ref[0] = _dot(h, w_ref[0].astype(BF16)) + b_ref[0]


def _adaln(cond, w_mod, b_mod):
    tn = TN_ADALN
    n = N_MOD * D_MODEL
    return pl.pallas_call(
        _adaln_kernel,
        grid=(DEPTH, n // tn),
        in_specs=[pl.BlockSpec((MOD_ROWS, D_MODEL), lambda i, j: (0, 0)),
                  pl.BlockSpec((1, D_MODEL, tn), lambda i, j: (i, 0, j)),
                  pl.BlockSpec((1, 1, tn), lambda i, j: (i, 0, j))],
        out_specs=pl.BlockSpec((1, MOD_ROWS, tn), lambda i, j: (i, 0, j)),
        out_shape=jax.ShapeDtypeStruct((DEPTH, MOD_ROWS, n), F32),
        compiler_params=_cparams("parallel", "parallel"),
        name="adaln",
    )(cond, w_mod, b_mod.reshape(DEPTH, 1, n))


def _modnorm(x, g, mod, shift_row, scale_row):
    y = _rms(x, D_MODEL) * g
    return y * (1.0 + mod[scale_row:scale_row + 1, :]) + mod[shift_row:shift_row + 1, :]


def _normproj_kernel(*refs, has_aux):
    if has_aux:
        x_ref, mod_ref, g_ref, w_ref, wa_ref, o_ref, oa_ref, h_ref = refs
    else:
        x_ref, mod_ref, g_ref, w_ref, o_ref, h_ref = refs

    j = pl.program_id(1)
    tn = o_ref.shape[1]

    @pl.when(j == 0)
    def _():
        h_ref[...] = _modnorm(x_ref[...], g_ref[...], mod_ref[0], 0, 1).astype(BF16)
        if has_aux:
            oa_ref[...] = _dot(h_ref[...], wa_ref[...])

    w = w_ref[:, pl.ds(pl.multiple_of(j * tn, tn), tn)]
    o_ref[...] = _dot(h_ref[...], w).astype(o_ref.dtype)


def _normproj(x, mod, g, w, rows_per_batch, tm, tn, out_dtype=F32, w_aux=None):
    t, n = x.shape[0], w.shape[1]
    in_specs = [pl.BlockSpec((tm, D_MODEL), lambda i, j: (i, 0)),
                pl.BlockSpec((1, MOD_PAD, D_MODEL), _mod_index(rows_per_batch, tm)),
                pl.BlockSpec((1, D_MODEL), lambda i, j: (0, 0)),
                pl.BlockSpec((D_MODEL, n), lambda i, j: (0, 0))]
    out_specs = [pl.BlockSpec((tm, tn), lambda i, j: (i, j))]
    out_shape = [jax.ShapeDtypeStruct((t, n), out_dtype)]
    args = [x, mod, g, w]
    if w_aux is not None:
        in_specs.append(pl.BlockSpec((D_MODEL, LANE), lambda i, j: (0, 0)))
        out_specs.append(pl.BlockSpec((tm, LANE), lambda i, j: (i, 0)))
        out_shape.append(jax.ShapeDtypeStruct((t, LANE), F32))
        args.append(w_aux)
    out = pl.pallas_call(
        functools.partial(_normproj_kernel, has_aux=w_aux is not None),
        grid=(t // tm, n // tn),
        in_specs=in_specs,
        out_specs=out_specs,
        out_shape=out_shape,
        scratch_shapes=[pltpu.VMEM((tm, D_MODEL), BF16)],
        compiler_params=_cparams("parallel", "arbitrary"),
        name="normproj",
    )(*args)
    return out if w_aux is not None else out[0]


def _mix_mlp_kernel(x_ref, a_ref, mod_ref, g_ref, wo_ref, win_ref, wout_ref, o_ref, h_ref, acc_ref):
    f = pl.program_id(1)

    @pl.when(f == 0)
    def _():
        o_ref[...] = x_ref[...] + mod_ref[0, 2:3, :] * _dot(a_ref[...], wo_ref[...])
        h_ref[...] = _modnorm(o_ref[...], g_ref[...], mod_ref[0], 3, 4).astype(BF16)
        acc_ref[...] = jnp.zeros_like(acc_ref)

    a = jnp.maximum(_dot(h_ref[...], win_ref[...]), 0.0)
    acc_ref[...] += _dot((a * a).astype(BF16), wout_ref[...])

    @pl.when(f == pl.num_programs(1) - 1)
    def _():
        o_ref[...] += mod_ref[0, 5:6, :] * acc_ref[...]


def _mix_mlp(x, a, mod, g, w_o, w_in, w_out, layer, rows_per_batch, tm, tf):
    t = x.shape[0]
    return pl.pallas_call(
        _mix_mlp_kernel,
        grid=(t // tm, D_FF // tf),
        in_specs=[pl.BlockSpec((tm, D_MODEL), lambda i, f: (i, 0)),
                  pl.BlockSpec((tm, D_MODEL), lambda i, f: (i, 0)),
                  pl.BlockSpec((1, MOD_PAD, D_MODEL), _mod_index(rows_per_batch, tm)),
                  pl.BlockSpec((1, D_MODEL), lambda i, f: (0, 0)),
                  pl.BlockSpec((D_MODEL, D_MODEL), lambda i, f: (0, 0)),
                  pl.BlockSpec((None, D_MODEL, tf), lambda i, f: (layer, 0, f)),
                  pl.BlockSpec((None, tf, D_MODEL), lambda i, f: (layer, f, 0))],
        out_specs=pl.BlockSpec((tm, D_MODEL), lambda i, f: (i, 0)),
        out_shape=jax.ShapeDtypeStruct((t, D_MODEL), F32),
        scratch_shapes=[pltpu.VMEM((tm, D_MODEL), BF16), pltpu.VMEM((tm, D_MODEL), F32)],
        compiler_params=_cparams("parallel", "arbitrary"),
        name="mix_mlp",
    )(x, a, mod, g, w_o, w_in, w_out)


def _attn_kernel(*refs, kv_per_step, group, dk, dv, has_ctx):
    if has_ctx:
        q_ref, k_ref, v_ref, kc_ref, vc_ref, o_ref = refs
    else:
        q_ref, k_ref, v_ref, o_ref = refs

    def scores(n):
        j = n // group
        q = q_ref[:, n * dk:(n + 1) * dk]
        s = _dot_nt(q, k_ref[:, j * dk:(j + 1) * dk])
        sc = _dot_nt(q, kc_ref[:, j * dk:(j + 1) * dk]) if has_ctx else None
        return s, sc

    heads = kv_per_step * group

    def with_ones(ref, j):
        v = ref[:, j * dv:(j + 1) * dv]
        return jnp.concatenate([v, jnp.ones_like(v)], axis=-1)

    va = [with_ones(v_ref, j) for j in range(kv_per_step)]
    vca = [with_ones(vc_ref, j) for j in range(kv_per_step)] if has_ctx else None
    nxt = scores(0)
    for n in range(heads):
        s, sc = nxt
        if n + 1 < heads:
            nxt = scores(n + 1)
        j = n // group
        m = jnp.max(s, axis=-1, keepdims=True)
        if has_ctx:
            m = jnp.maximum(m, jnp.max(sc, axis=-1, keepdims=True))
            pc = jnp.exp2(sc - m)
        p = jnp.exp2(s - m)
        ol = _dot(p.astype(BF16), va[j])
        if has_ctx:
            ol = ol + _dot(pc.astype(BF16), vca[j])
        o_ref[:, n * dv:(n + 1) * dv] = (ol[:, :dv] / ol[:, dv:dv + 1]).astype(o_ref.dtype)


def _attention(q, k, v, ctx, *, batch, seq, heads_kv, kv_per_step, group, dk, dv, tq):
    tq = min(tq, seq)
    nq = seq // tq
    hp = kv_per_step
    in_specs = [pl.BlockSpec((tq, hp * group * dk), lambda b, j, i: (b * nq + i, j)),
                pl.BlockSpec((seq, hp * dk), lambda b, j, i: (b, j)),
                pl.BlockSpec((seq, hp * dv), lambda b, j, i: (b, j))]
    args = [q, k, v]
    if ctx is not None:
        past = ctx[0].shape[0] // batch
        in_specs += [pl.BlockSpec((past, hp * dk), lambda b, j, i: (b, j)),
                     pl.BlockSpec((past, hp * dv), lambda b, j, i: (b, j))]
        args += list(ctx)
    return pl.pallas_call(
        functools.partial(_attn_kernel, kv_per_step=hp, group=group, dk=dk, dv=dv, has_ctx=ctx is not None),
        grid=(batch, heads_kv // hp, nq),
        in_specs=in_specs,
        out_specs=pl.BlockSpec((tq, hp * group * dv), lambda b, j, i: (b * nq + i, j)),
        out_shape=jax.ShapeDtypeStruct((batch * seq, heads_kv * group * dv), BF16),
        compiler_params=_cparams("parallel", "parallel", "parallel"),
        name="attention",
    )(*args)


def _swap_quarters(x, quarter):
    lanes = x.shape[-1]
    lane = lax.broadcasted_iota(jnp.int32, x.shape, x.ndim - 1)
    up = pltpu.roll(x, lanes - quarter, x.ndim - 1)
    down = pltpu.roll(x, quarter, x.ndim - 1)
    return jnp.where((lane // quarter) % 2 == 0, up, down)


def _rope(x, cos, sin, quarter):
    return x * cos + _swap_quarters(x, quarter) * sin


def _rope_tables(seq, width, pad_to):
    rows = seq // GRID_W
    row = jnp.repeat(jnp.arange(rows, dtype=F32), GRID_W)
    col = jnp.tile(jnp.arange(GRID_W, dtype=F32), rows)
    quarter = width // 4
    freqs = ROPE_THETA ** (-jnp.arange(quarter, dtype=F32) / quarter)
    ar = row[:, None] * freqs[None, :]
    ac = col[:, None] * freqs[None, :]
    cos = jnp.concatenate([jnp.cos(ar), jnp.cos(ar), jnp.cos(ac), jnp.cos(ac)], axis=-1)
    sin = jnp.concatenate([-jnp.sin(ar), jnp.sin(ar), -jnp.sin(ac), jnp.sin(ac)], axis=-1)
    pad = ((0, 0), (0, pad_to - width))
    return jnp.pad(cos, pad), jnp.pad(sin, pad)


def _gqa_prep_kernel(*refs, use_rope):
    if use_rope:
        x_ref, mod_ref, g_ref, w_ref, qg_ref, kg_ref, cos_ref, sin_ref, q_ref, k_ref, v_ref, kf_ref, vf_ref = refs
    else:
        x_ref, mod_ref, g_ref, w_ref, qg_ref, kg_ref, q_ref, k_ref, v_ref, kf_ref, vf_ref = refs
    proj = _dot(_modnorm(x_ref[...], g_ref[...], mod_ref[0], 0, 1).astype(BF16), w_ref[...])
    for h in range(H_C + KVH_C):
        x = _rms(proj[:, h * HD_C:(h + 1) * HD_C], HD_C)
        x = x * (qg_ref[...] if h < H_C else kg_ref[...])
        if h >= H_C:
            kf_ref[:, (h - H_C) * HD_C:(h - H_C + 1) * HD_C] = x
        if use_rope:
            x = _rope(x, cos_ref[...], sin_ref[...], HD_C // 4)
        if h < H_C:
            q_ref[:, h * HD_C:(h + 1) * HD_C] = (x * (LOG2_E * HD_C ** -0.5)).astype(BF16)
        else:
            k_ref[:, (h - H_C) * HD_C:(h - H_C + 1) * HD_C] = x.astype(BF16)
    vv = proj[:, (H_C + KVH_C) * HD_C:]
    vf_ref[...] = vv
    v_ref[...] = vv.astype(BF16)


def _gqa_prep(x, mod, g, w, q_g, k_g, tables, rows_per_batch, seq, tm):
    t = x.shape[0]
    nq, nkv = H_C * HD_C, KVH_C * HD_C
    in_specs = [pl.BlockSpec((tm, D_MODEL), lambda i: (i, 0)),
                pl.BlockSpec((1, MOD_PAD, D_MODEL), _mod_index(rows_per_batch, tm)),
                pl.BlockSpec((1, D_MODEL), lambda i: (0, 0)),
                pl.BlockSpec((D_MODEL, nq + 2 * nkv), lambda i: (0, 0)),
                pl.BlockSpec((1, HD_C), lambda i: (0, 0)),
                pl.BlockSpec((1, HD_C), lambda i: (0, 0))]
    args = [x, mod, g, w, q_g, k_g]
    if tables is not None:
        per = seq // tm
        in_specs += [pl.BlockSpec((tm, HD_C), lambda i: (i % per, 0))] * 2
        args += list(tables)
    return pl.pallas_call(
        functools.partial(_gqa_prep_kernel, use_rope=tables is not None),
        grid=(t // tm,),
        in_specs=in_specs,
        out_specs=[pl.BlockSpec((tm, nq), lambda i: (i, 0))] + [pl.BlockSpec((tm, nkv), lambda i: (i, 0))] * 4,
        out_shape=[jax.ShapeDtypeStruct((t, nq), BF16), jax.ShapeDtypeStruct((t, nkv), BF16),
                   jax.ShapeDtypeStruct((t, nkv), BF16), jax.ShapeDtypeStruct((t, nkv), F32),
                   jax.ShapeDtypeStruct((t, nkv), F32)],
        compiler_params=_cparams("parallel"),
        name="gqa_prep",
    )(*args)


def _mla_keys_values(ckv, krope_slot, wukv_ref, knn_ref, k_ref, v_ref):
    kv = _dot(ckv.astype(BF16), wukv_ref[...])
    kr = krope_slot.astype(BF16)
    for h in range(H_B):
        base = h * (NOPE_B + V_B)
        kn = _rms(kv[:, base:base + NOPE_B], NOPE_B) * knn_ref[...]
        k_ref[:, h * SLOT_B:h * SLOT_B + NOPE_B] = kn.astype(BF16)
        k_ref[:, h * SLOT_B + NOPE_B:(h + 1) * SLOT_B] = kr
        v_ref[:, h * V_B:(h + 1) * V_B] = kv[:, base + NOPE_B:base + NOPE_B + V_B].astype(BF16)


def _mla_prep_kernel(*refs, use_rope):
    if use_rope:
        (p_ref, qlg_ref, kvg_ref, krg_ref, wuq_ref, wukv_ref, qnn_ref, qnr_ref, knn_ref, cos_ref, sin_ref,
         q_ref, k_ref, v_ref, ckv_ref, kr_ref) = refs
    else:
        (p_ref, qlg_ref, kvg_ref, krg_ref, wuq_ref, wukv_ref, qnn_ref, qnr_ref, knn_ref,
         q_ref, k_ref, v_ref, ckv_ref, kr_ref) = refs
    cq = _rms(p_ref[:, :Q_LORA], Q_LORA) * qlg_ref[...]
    ckv = _rms(p_ref[:, Q_LORA:Q_LORA + KV_LORA], KV_LORA) * kvg_ref[...]
    kr = _rms(p_ref[:, Q_LORA + KV_LORA:], ROPE_B) * krg_ref[...]
    ckv_ref[...] = ckv
    kr_ref[...] = kr
    q = _dot(cq.astype(BF16), wuq_ref[...])
    scale = LOG2_E * (NOPE_B + ROPE_B) ** -0.5
    for h in range(H_B):
        qn = _rms(q[:, h * SLOT_B:h * SLOT_B + NOPE_B], NOPE_B) * qnn_ref[...]
        qr = _rms(q[:, h * SLOT_B + NOPE_B:(h + 1) * SLOT_B], ROPE_B) * qnr_ref[...]
        if use_rope:
            qr = _rope(qr, cos_ref[...], sin_ref[...], ROPE_B // 4)
        q_ref[:, h * SLOT_B:h * SLOT_B + NOPE_B] = (qn * scale).astype(BF16)
        q_ref[:, h * SLOT_B + NOPE_B:(h + 1) * SLOT_B] = (qr * scale).astype(BF16)
    if use_rope:
        kr = _rope(kr, cos_ref[...], sin_ref[...], ROPE_B // 4)
    _mla_keys_values(ckv, kr, wukv_ref, knn_ref, k_ref, v_ref)


def _mla_prep(proj, p, tables, seq, tm):
    t = proj.shape[0]
    full = lambda shape: pl.BlockSpec(shape, lambda i: (0, 0))
    in_specs = [pl.BlockSpec((tm, DOWN_PAD), lambda i: (i, 0)),
                full((1, Q_LORA)), full((1, KV_LORA)), full((1, LANE)),
                full((Q_LORA, H_B * SLOT_B)), full((KV_LORA, H_B * (NOPE_B + V_B))),
                full((1, NOPE_B)), full((1, LANE)), full((1, NOPE_B))]
    args = [proj, p["q_lat_g"], p["kv_lat_g"], p["kn_rope"], p["w_uq"], p["w_ukv"], p["qn_nope"], p["qn_rope"],
            p["kn_nope"]]
    if tables is not None:
        per = seq // tm
        in_specs += [pl.BlockSpec((tm, LANE), lambda i: (i % per, 0))] * 2
        args += list(tables)
    row = lambda n: pl.BlockSpec((tm, n), lambda i: (i, 0))
    return pl.pallas_call(
        functools.partial(_mla_prep_kernel, use_rope=tables is not None),
        grid=(t // tm,),
        in_specs=in_specs,
        out_specs=[row(H_B * SLOT_B), row(H_B * SLOT_B), row(H_B * V_B), row(KV_LORA), row(LANE)],
        out_shape=[jax.ShapeDtypeStruct((t, H_B * SLOT_B), BF16), jax.ShapeDtypeStruct((t, H_B * SLOT_B), BF16),
                   jax.ShapeDtypeStruct((t, H_B * V_B), BF16), jax.ShapeDtypeStruct((t, KV_LORA), F32),
                   jax.ShapeDtypeStruct((t, LANE), F32)],
        compiler_params=_cparams("parallel"),
        name="mla_prep",
    )(*args)


def _mla_ctx_kernel(ckv_ref, kr_ref, wukv_ref, knn_ref, k_ref, v_ref):
    _mla_keys_values(ckv_ref[...], kr_ref[...], wukv_ref, knn_ref, k_ref, v_ref)


def _mla_ctx(ckv, krope_slot, p, tm):
    t = ckv.shape[0]
    full = lambda shape: pl.BlockSpec(shape, lambda i: (0, 0))
    row = lambda n: pl.BlockSpec((tm, n), lambda i: (i, 0))
    return pl.pallas_call(
        _mla_ctx_kernel,
        grid=(t // tm,),
        in_specs=[row(KV_LORA), row(LANE), full((KV_LORA, H_B * (NOPE_B + V_B))), full((1, NOPE_B))],
        out_specs=[row(H_B * SLOT_B), row(H_B * V_B)],
        out_shape=[jax.ShapeDtypeStruct((t, H_B * SLOT_B), BF16), jax.ShapeDtypeStruct((t, H_B * V_B), BF16)],
        compiler_params=_cparams("parallel"),
        name="mla_ctx",
    )(ckv, krope_slot, p["w_ukv"], p["kn_nope"])


def _silu(y):
    h = 0.5 * y
    return h * jnp.tanh(h) + h


def _conv_silu(x, w):
    s = x.shape[0]
    row = lax.broadcasted_iota(jnp.int32, x.shape, 0)
    prev = jnp.where(row == 0, 0.0, pltpu.roll(x, 1, 0))
    nxt = jnp.where(row == s - 1, 0.0, pltpu.roll(x, s - 1, 0))
    y = prev * w[0:1, :] + x * w[1:2, :] + nxt * w[2:3, :]
    return _silu(y)


def _l2n(x):
    return x * lax.rsqrt(jnp.sum(x * x, axis=-1, keepdims=True) + EPS)


def _softplus(x):
    return jnp.maximum(x, 0.0) + jnp.log1p(jnp.exp(-jnp.abs(x)))


GDN_CHUNK_HEADS = 64
GDN_PACK = 4
GDN_UNROLL = 16


def _gdn_kernel(alog_ref, dtb_ref, q_ref, k_ref, v_ref, z_ref, cq_ref, ck_ref, cv_ref, gate_ref,
                s0f_ref, s0b_ref, on_ref, o_ref, sf_ref, sb_ref,
                qs, ks, vs, obuf, qeff, nmat, mneg, egt, st, *, seq, heads, unroll, pack):
    h0 = pl.program_id(1) * heads
    n_chunks = seq // CHUNK
    for hh in range(heads):
        cols = slice(hh * DK_A, (hh + 1) * DK_A)
        qs[hh] = _l2n(_conv_silu(q_ref[:, cols].astype(F32), cq_ref[:, cols])) * (DK_A ** -0.5)
        ks[hh] = _l2n(_conv_silu(k_ref[:, cols].astype(F32), ck_ref[:, cols]))
        vs[hh] = _conv_silu(v_ref[:, cols].astype(F32), cv_ref[:, cols])

    width = pack * CHUNK
    ii = lax.broadcasted_iota(jnp.int32, (CHUNK, CHUNK), 0)
    jj = lax.broadcasted_iota(jnp.int32, (CHUNK, CHUNK), 1)
    eye = ii == jj
    ip = lax.broadcasted_iota(jnp.int32, (CHUNK, width), 0)
    lane = lax.broadcasted_iota(jnp.int32, (CHUNK, width), 1)
    jp = lane % CHUNK
    in_blk = [(lane // CHUNK) == i for i in range(pack)]
    eye_p = ip == jp
    incl = (ii >= jj, ii <= jj)
    incl_p = (ip >= jp, ip <= jp)
    strict_p = (ip > jp, ip < jp)
    last = (CHUNK - 1, 0)
    pair_p = [(ip // 2) == (jp // 2)]
    pair_p += [((ip // (2 * s)) == (jp // (2 * s))) & ((ip // s) != (jp // s)) for s in (2, 4, 8, 16, 32)]
    slots = [(hh, d) for hh in range(heads) for d in range(2)]

    def pack_cols(cols):
        out = jnp.broadcast_to(cols[0], (CHUNK, width))
        for i in range(1, pack):
            out = jnp.where(in_blk[i], cols[i], out)
        return out

    def pack_diag(g):
        out = g[:CHUNK, :]
        for i in range(1, pack):
            out = jnp.where(in_blk[i], g[i * CHUNK:(i + 1) * CHUNK, :], out)
        return out

    def block_diag(x):
        return jnp.concatenate([jnp.where(in_blk[i], x, jnp.zeros_like(x)) for i in range(pack)], axis=0)
    neg_a = [-jnp.exp(jnp.full((1, CHUNK), alog_ref[d, h0 + hh], F32)) for hh, d in slots]
    dtb = [dtb_ref[d, h0 + hh] for hh, d in slots]

    def to_col(r):
        return jnp.sum(jnp.where(eye, jnp.broadcast_to(r, (CHUNK, CHUNK)), 0.0), axis=1, keepdims=True)

    def prepare(blk, carry):
        c0s = [blk * unroll + p * pack for p in range(unroll // pack)]
        offs = [pl.multiple_of(c0 * CHUNK, width) for c0 in c0s]
        tiles = [(hh, p) for hh in range(heads) for p in range(len(c0s))]
        qt = {t: qs[t[0], pl.ds(offs[t[1]], width), :] for t in tiles}
        kt = {t: ks[t[0], pl.ds(offs[t[1]], width), :] for t in tiles}
        vt = {t: vs[t[0], pl.ds(offs[t[1]], width), :] for t in tiles}
        ktb = {t: kt[t].astype(BF16) for t in tiles}
        kk = {t: pack_diag(_dot_nt(ktb[t], ktb[t])) for t in tiles}
        qk_raw = {t: pack_diag(_dot_nt(qt[t].astype(BF16), ktb[t])) for t in tiles}
        groups = [(hh, p, d) for hh, p in tiles for d in range(2)]
        a_mat, qk, beta_c, gcum_c, g_tot = [], [], [], [], []
        for hh, p, d in groups:
            slot = 2 * hh + d
            cols_beta, cols_gcum, tot = [], [], []
            for i in range(pack):
                g_raw = gate_ref[0, d * H_A + h0 + hh, pl.ds(c0s[p] + i, 1), :]
                b_raw = gate_ref[0, 2 * H_A + d * H_A + h0 + hh, pl.ds(c0s[p] + i, 1), :]
                g_row = neg_a[slot] * _softplus(g_raw + dtb[slot])
                gcum = jnp.sum(jnp.where(incl[d], jnp.broadcast_to(g_row, (CHUNK, CHUNK)), 0.0), axis=1,
                               keepdims=True)
                cols_gcum.append(gcum)
                cols_beta.append(to_col(jax.nn.sigmoid(b_raw)))
                tot.append(gcum[last[d]:last[d] + 1, :])
            gcum_p = pack_cols(cols_gcum)
            gcum_row = jnp.sum(jnp.where(eye_p, gcum_p, 0.0), axis=0, keepdims=True)
            decay = jnp.exp(jnp.where(incl_p[d], gcum_p - gcum_row, -jnp.inf))
            a_mat.append(jnp.where(strict_p[d], kk[hh, p] * pack_cols(cols_beta) * decay, 0.0))
            qk.append((qk_raw[hh, p] * decay).astype(BF16))
            beta_c.append(cols_beta)
            gcum_c.append(cols_gcum)
            g_tot.append(tot)
        tm = [jnp.where(eye_p, 1.0, -jnp.where(pair_p[0], a, 0.0)) for a in a_mat]
        for lvl in range(1, len(pair_p)):
            tb = [t.astype(BF16) for t in tm]
            x = [_dot(jnp.where(pair_p[lvl], a, 0.0).astype(BF16), block_diag(t)) for a, t in zip(a_mat, tb)]
            tm = [t - _dot(t16, block_diag(xi.astype(BF16))) for t, t16, xi in zip(tm, tb, x)]
        nm = [jnp.where(eye_p, 0.0, t).astype(BF16) for t in tm]
        rows = [slice(i * CHUNK, (i + 1) * CHUNK) for i in range(pack)]
        eg_c = [[jnp.exp(x) for x in cols] for cols in gcum_c]
        rhs = [jnp.concatenate(
            [jnp.concatenate([vt[hh, p][rows[i]] * beta_c[g][i], kt[hh, p][rows[i]] * (beta_c[g][i] * eg_c[g][i])],
                             axis=-1) for i in range(pack)], axis=0)
            for g, (hh, p, d) in enumerate(groups)]
        rhsb = [r.astype(BF16) for r in rhs]
        zero_p = jnp.zeros((CHUNK, width), BF16)
        sol = [jnp.concatenate([(rhs[g][rows[i]] + _dot(jnp.where(in_blk[i], nm[g], zero_p), rhsb[g])).astype(BF16)
                                for i in range(pack)], axis=0) for g in range(len(groups))]
        local = [jnp.concatenate([_dot(jnp.where(in_blk[i], qk[g], zero_p), sol[g]) for i in range(pack)], axis=0)
                 for g in range(len(groups))]
        k_dec = [[(kt[hh, p][rows[i]] * jnp.exp(g_tot[g][i] - gcum_c[g][i])).astype(BF16) for i in range(pack)]
                 for g, (hh, p, d) in enumerate(groups)]
        nm_mat = [[_dot_tn(k_dec[g][i], sol[g][rows[i]]) for i in range(pack)]
                  for g in range(len(groups))]
        for g, (hh, p, d) in enumerate(groups):
            slot = 2 * hh + d
            obuf[slot, pl.ds(offs[p], width), :] = local[g][:, :DV_A]
            q_eff = [qt[hh, p][rows[i]] * eg_c[g][i] - local[g][rows[i], DV_A:] for i in range(pack)]
            qeff[slot, pl.ds(offs[p], width), :] = jnp.concatenate(q_eff, axis=0).astype(BF16)
            for i in range(pack):
                nmat[slot, c0s[p] + i] = nm_mat[g][i][:, :DV_A]
                mneg[slot, c0s[p] + i] = (-nm_mat[g][i][:, DV_A:]).astype(BF16)
                egt[slot, c0s[p] + i] = jnp.broadcast_to(jnp.exp(g_tot[g][i]), (8, DV_A))
        return carry

    lax.fori_loop(0, n_chunks // unroll, prepare, 0)

    for hh, d in slots:
        st[2 * hh + d] = (s0f_ref, s0b_ref)[d][0, hh]

    def scan(ci, carry):
        states = [st[slot] for slot in range(len(slots))]
        sb16 = [s.astype(BF16) for s in states]
        cs = [ci if d == 0 else n_chunks - 1 - ci for hh, d in slots]
        for slot, c in enumerate(cs):
            st[slot] = states[slot] * egt[slot, c][0:1, :] + _dot(mneg[slot, c], sb16[slot]) + nmat[slot, c]
        for slot, c in enumerate(cs):
            off = pl.multiple_of(c * CHUNK, CHUNK)
            obuf[slot, pl.ds(off, CHUNK), :] += _dot(qeff[slot, pl.ds(off, CHUNK), :], sb16[slot])
        return carry

    lax.fori_loop(0, n_chunks, scan, 0, unroll=8 if n_chunks % 8 == 0 else 4)
    for hh, d in slots:
        (sf_ref, sb_ref)[d][0, hh] = st[2 * hh + d]

    for hh in range(heads):
        cols = slice(hh * DV_A, (hh + 1) * DV_A)
        z =z_ref[:, cols].astype(F32)
        o = _rms(obuf[2 * hh] + obuf[2 * hh + 1], DV_A) * on_ref[...] * _silu(z)
        o_ref[:, cols] = o.astype(BF16)


def _gdn(proj, gates, conv_w, a_log, dt_bias, out_norm, s0f, s0b, batch, seq):
    n_chunks = seq // CHUNK
    unroll = math.gcd(n_chunks, GDN_UNROLL)
    pack = math.gcd(unroll, GDN_PACK if unroll > GDN_PACK else GDN_PACK // 2)
    hb = min(4, max(2, GDN_CHUNK_HEADS // n_chunks))
    groups, slots = H_A // hb, 2 * hb
    smem = pl.BlockSpec(memory_space=pltpu.SMEM)
    col = lambda off: pl.BlockSpec((seq, hb * DK_A), lambda b, h, off=off: (b, off + h))
    cw = lambda off: pl.BlockSpec((3, hb * DK_A), lambda b, h, off=off: (0, off + h))
    state = pl.BlockSpec((1, hb, DK_A, DV_A), lambda b, h: (b, h, 0, 0))
    st_shape = jax.ShapeDtypeStruct((batch, H_A, DK_A, DV_A), F32)
    return pl.pallas_call(
        functools.partial(_gdn_kernel, seq=seq, heads=hb, unroll=unroll, pack=pack),
        grid=(batch, groups),
        in_specs=[smem, smem, col(0), col(groups), col(2 * groups), col(3 * groups),
                  cw(0), cw(groups), cw(2 * groups),
                  pl.BlockSpec((1, 4 * H_A, n_chunks, CHUNK), lambda b, h: (b, 0, 0, 0)),
                  state, state, pl.BlockSpec((1, DV_A), lambda b, h: (0, 0))],
        out_specs=[pl.BlockSpec((seq, hb * DV_A), lambda b, h: (b, h)), state, state],
        out_shape=[jax.ShapeDtypeStruct((batch * seq, WV_A), BF16), st_shape, st_shape],
        scratch_shapes=[pltpu.VMEM((hb, seq, DK_A), F32), pltpu.VMEM((hb, seq, DK_A), F32),
                        pltpu.VMEM((hb, seq, DV_A), F32),
                        pltpu.VMEM((slots, seq, DV_A), F32), pltpu.VMEM((slots, seq, DK_A), BF16),
                        pltpu.VMEM((slots, n_chunks, DK_A, DV_A), F32),
                        pltpu.VMEM((slots, n_chunks, DK_A, DK_A), BF16),
                        pltpu.VMEM((slots, n_chunks, 8, DV_A), F32), pltpu.VMEM((slots, DK_A, DV_A), F32)],
        compiler_params=_cparams("parallel", "parallel"),
        name="gdn",
    )(a_log, dt_bias, proj, proj, proj, proj, conv_w, conv_w, conv_w, gates, s0f, s0b, out_norm)


def _pad_cols(w, n):
    return jnp.pad(w, ((0, 0), (0, n - w.shape[1])))


def kernel(x_prompt, x_sample, state_gdn_fwd, state_gdn_bwd, cache_mla_ckv, cache_mla_krope, cache_gqa_k, cache_gqa_v, c, c_ctx, norm_mix, norm_mlp, w_mod, b_mod, w_mlp_in, w_mlp_out, gdn_w_in, gdn_conv, gdn_a_log, gdn_dt_bias, gdn_out_norm, gdn_w_out, mla_w_down, mla_q_lat_norm, mla_kv_lat_norm, mla_w_uq, mla_w_ukv, mla_qn_nope, mla_qn_rope, mla_kn_nope, mla_kn_rope, mla_w_out, gqa_w_in, gqa_q_norm, gqa_k_norm, gqa_w_out):
    bp, sp, d = x_prompt.shape
    bs, ss, _ = x_sample.shape
    past = cache_mla_ckv.shape[2]
    xs = [x_prompt.reshape(bp * sp, d), x_sample.reshape(bs * ss, d)]
    batches, seqs = (bp, bs), (sp, ss)
    rows_per_batch = (None, ss)
    tms = (TM_TOKENS, TM_TOKENS)

    cond = jnp.zeros((MOD_ROWS, d), F32).at[:bs].set(c).at[CTX_ROW].set(c_ctx)
    mods = _adaln(cond, w_mod, b_mod).reshape(DEPTH, MOD_ROWS, N_MOD, d)
    mods = jnp.pad(mods, ((0, 0), (0, 0), (0, MOD_PAD - N_MOD), (0, 0)))

    gqa_tables = _rope_tables(ss, HD_C, HD_C)
    mla_tables = _rope_tables(ss, ROPE_B, LANE)

    w_mi, w_mo = w_mlp_in.astype(BF16), w_mlp_out.astype(BF16)
    new_gdn_f, new_gdn_b, new_ckv, new_kr, new_k, new_v = [], [], [], [], [], []
    for i in range(DEPTH):
        kind, j = i % N_MIXERS, i // N_MIXERS
        mod = mods[i]
        g_mix = norm_mix[i].reshape(1, d)
        g_mlp = norm_mlp[i].reshape(1, d)
        mixed = []
        if kind == 0:
            w_in = gdn_w_in[j]
            n_main = 2 * WK_A + 2 * WV_A
            w_main = w_in[:, :n_main].astype(BF16)
            w_gate = _pad_cols(w_in[:, n_main:], LANE).astype(BF16)
            w_o = gdn_w_out[j].astype(BF16)
            on = gdn_out_norm[j].reshape(1, DV_A)
            for gi in range(2):
                b, s, tm = batches[gi], seqs[gi], tms[gi]
                proj, gb = _normproj(xs[gi], mod, g_mix, w_main, rows_per_batch[gi], tm, TN_PROJ, BF16, w_gate)
                gates = gb[:, :4 * H_A].reshape(b, s // CHUNK, CHUNK, 4 * H_A).transpose(0, 3, 1, 2)
                if gi == 0:
                    s0f = s0b = jnp.zeros((b, H_A, DK_A, DV_A), F32)
                else:
                    s0f, s0b = state_gdn_fwd[:, j], state_gdn_bwd[:, j]
                o, s_f, s_b = _gdn(proj, gates, gdn_conv[j], gdn_a_log[j], gdn_dt_bias[j], on, s0f, s0b, b, s)
                if gi == 0:
                    new_gdn_f.append(s_f)
                    new_gdn_b.append(s_b)
                mixed.append((o, w_o))
        elif kind == 1:
            w_uq = mla_w_uq[j].reshape(Q_LORA, H_B, NOPE_B + ROPE_B)
            w_uq = jnp.pad(w_uq, ((0, 0), (0, 0), (0, SLOT_B - NOPE_B - ROPE_B))).reshape(Q_LORA, H_B * SLOT_B)
            p = dict(q_lat_g=mla_q_lat_norm[j].reshape(1, Q_LORA), kv_lat_g=mla_kv_lat_norm[j].reshape(1, KV_LORA),
                     kn_rope=_pad_cols(mla_kn_rope[j].reshape(1, ROPE_B), LANE),
                     qn_rope=_pad_cols(mla_qn_rope[j].reshape(1, ROPE_B), LANE),
                     qn_nope=mla_qn_nope[j].reshape(1, NOPE_B), kn_nope=mla_kn_nope[j].reshape(1, NOPE_B),
                     w_uq=w_uq.astype(BF16), w_ukv=mla_w_ukv[j].astype(BF16))
            w_down = _pad_cols(mla_w_down[j], DOWN_PAD).astype(BF16)
            w_o = mla_w_out[j].astype(BF16)
            for gi in range(2):
                b, s, tm = batches[gi], seqs[gi], tms[gi]
                proj = _normproj(xs[gi], mod, g_mix, w_down, rows_per_batch[gi], tm, DOWN_PAD)
                q, k, v, ckv, kr = _mla_prep(proj, p, mla_tables if gi == 1 else None, s, TM_PREP)
                ctx = None
                if gi == 0:
                    new_ckv.append(ckv.reshape(b, s, KV_LORA))
                    new_kr.append(kr[:, :ROPE_B].reshape(b, s, ROPE_B))
                else:
                    ckv_c = cache_mla_ckv[:, j].reshape(b * past, KV_LORA)
                    kr_c = _pad_cols(cache_mla_krope[:, j].reshape(b * past, ROPE_B), LANE)
                    ctx = _mla_ctx(ckv_c, kr_c, p, TM_PREP)
                o = _attention(q, k, v, ctx, batch=b, seq=s, heads_kv=H_B, kv_per_step=4, group=1, dk=SLOT_B,
                               dv=V_B, tq=TQ_ATTN)
                mixed.append((o, w_o))
        else:
            w_in = gqa_w_in[j].astype(BF16)
            w_o = gqa_w_out[j].astype(BF16)
            q_g, k_g = gqa_q_norm[j].reshape(1, HD_C), gqa_k_norm[j].reshape(1, HD_C)
            for gi in range(2):
                b, s, tm = batches[gi], seqs[gi], tms[gi]
                q, k, v, kf, vf = _gqa_prep(xs[gi], mod, g_mix, w_in, q_g, k_g, gqa_tables if gi == 1 else None,
                                            rows_per_batch[gi], s, TM_PREP)
                ctx = None
                if gi == 0:
                    new_k.append(kf.reshape(b, s, KVH_C, HD_C))
                    new_v.append(vf.reshape(b, s, KVH_C, HD_C))
                else:
                    ctx = (cache_gqa_k[:, j].reshape(b * past, KVH_C * HD_C).astype(BF16),
                           cache_gqa_v[:, j].reshape(b * past, KVH_C * HD_C).astype(BF16))
                o = _attention(q, k, v, ctx, batch=b, seq=s, heads_kv=KVH_C, kv_per_step=2, group=H_C // KVH_C,
                               dk=HD_C, dv=HD_C, tq=TQ_ATTN)
                mixed.append((o, w_o))
        for gi in range(2):
            o, w_o = mixed[gi]
            xs[gi] = _mix_mlp(xs[gi], o, mod, g_mlp, w_o, w_mi, w_mo, i, rows_per_batch[gi], tms[gi], TF_MLP)

    dt = x_prompt.dtype
    stack = lambda lst: jnp.stack(lst, axis=1).astype(dt)
    return (xs[0].reshape(bp, sp, d), xs[1].reshape(bs, ss, d), stack(new_gdn_f), stack(new_gdn_b),
            stack(new_ckv), stack(new_kr), stack(new_k), stack(new_v))
```
